```python
import math
import jax, jax.numpy as jnp
from jax import lax
import numpy as np

D_MODEL = 1024
BATCH = 8
SEQ = 2048
DEPTH = 4

N_MIXERS = 3
N_ATTN_LAYERS = len(range(0, DEPTH, N_MIXERS))
N_CONV_LAYERS = len(range(1, DEPTH, N_MIXERS))
N_SGU_LAYERS = len(range(2, DEPTH, N_MIXERS))

ATTN_HEADS = 8
ATTN_HEAD_DIM = D_MODEL // (2 * ATTN_HEADS)
Q_BLOCK = 128
REL_BUCKETS = 32
REL_MAX_DIST = 128
CONV_WIDTH = 3
SGU_CHUNK = 128
SGU_GROUPS = 8
SGU_GROUP_DIM = D_MODEL // SGU_GROUPS
PEER_HEADS = 8
PEER_NKEYS = 128
PEER_EXPERTS = PEER_NKEYS * PEER_NKEYS
PEER_KEY_DIM = 256
PEER_HALF = PEER_KEY_DIM // 2
PEER_TOPK = 16
PEER_TOKEN_BLOCK = 128
EPS = 1e-6

kernel_name = "hybrid_diffattn_shortconv_gmlp_peer"

F32 = jnp.float32


def rmsnorm(x, g):
    xf = x.astype(F32)
    y = xf * lax.rsqrt(jnp.mean(xf * xf, axis=-1, keepdims=True) + EPS)
    return (y * g.astype(F32)).astype(x.dtype)


def rel_bucket(dist):
    n = jnp.maximum(dist, 0)
    max_exact = REL_BUCKETS // 2
    nf = jnp.maximum(n, 1).astype(F32)
    large = max_exact + (jnp.log(nf / max_exact) / math.log(REL_MAX_DIST / max_exact)
                         * (REL_BUCKETS - max_exact)).astype(jnp.int32)
    large = jnp.minimum(large, REL_BUCKETS - 1)
    return jnp.where(n < max_exact, n, large)


def lambda_init_fn(layer_idx):
    return 0.8 - 0.6 * math.exp(-0.3 * layer_idx)


def diff_attention(h, w_qkv, q_gain, k_gain, lam_q1, lam_k1, lam_q2, lam_k2, sub_gain, w_o,
                   rel_bias, lam_init):
    B, S, D = h.shape
    H, DH = ATTN_HEADS, ATTN_HEAD_DIM
    qkv = h @ w_qkv
    q = rmsnorm(qkv[..., :D].reshape(B, S, H, 2, DH), q_gain)
    k = rmsnorm(qkv[..., D:2 * D].reshape(B, S, H, 2, DH), k_gain)
    v = qkv[..., 2 * D:].reshape(B, S, H, 2 * DH)
    lam = (jnp.exp(jnp.sum(lam_q1.astype(F32) * lam_k1.astype(F32)))
           - jnp.exp(jnp.sum(lam_q2.astype(F32) * lam_k2.astype(F32))) + lam_init)
    scale = DH ** -0.5
    pos = jnp.arange(S, dtype=jnp.int32)
    outs = []
    for blk in range(S // Q_BLOCK):
        start = blk * Q_BLOCK
        end = start + Q_BLOCK
        dist = pos[start:end, None] - pos[None, :end]
        bias = jnp.transpose(rel_bias[rel_bucket(dist)], (2, 0, 1)).astype(F32)
        logits = (jnp.einsum('bqhmd,bkhmd->bhmqk', q[:, start:end], k[:, :end]).astype(F32) * scale
                  + bias[None, :, None])
        logits = jnp.where(dist >= 0, logits, -jnp.inf)
        p = jax.nn.softmax(logits, axis=-1)
        attn = (p[:, :, 0] - lam * p[:, :, 1]).astype(v.dtype)
        outs.append(jnp.einsum('bhqk,bkhe->bqhe', attn, v[:, :end]))
    o = jnp.concatenate(outs, axis=1)
    o = rmsnorm(o, sub_gain) * (1.0 - lam_init)
    return o.reshape(B, S, D) @ w_o


def short_conv(h, w_in, conv_w, w_out):
    D = h.shape[-1]
    bcx = h @ w_in
    b_gate, c_gate, xin = bcx[..., :D], bcx[..., D:2 * D], bcx[..., 2 * D:]
    z = c_gate * xin
    zc = lax.conv_general_dilated(z, conv_w[:, None, :].astype(z.dtype), window_strides=(1,),
                                  padding=[(CONV_WIDTH - 1, 0)],
                                  dimension_numbers=('NWC', 'WIO', 'NWC'),
                                  feature_group_count=D)
    return (b_gate * zc) @ w_out


def spatial_gating_mlp(h, w_in, v_gain, w_s, b_s, w_out):
    B, S, D = h.shape
    z = jax.nn.gelu(h @ w_in)
    u, v = z[..., :D], rmsnorm(z[..., D:], v_gain)
    v = v.reshape(B, S // SGU_CHUNK, SGU_CHUNK, SGU_GROUPS, SGU_GROUP_DIM)
    mask = jnp.tril(jnp.ones((SGU_CHUNK, SGU_CHUNK), dtype=bool))
    w = jnp.where(mask[None], w_s, 0.0).astype(v.dtype)
    sv = (jnp.einsum('gts,bnsgc->bntgc', w, v)
          + jnp.transpose(b_s, (1, 0))[None, None, :, :, None])
    return (u * sv.reshape(B, S, D)) @ w_out


def peer_ffn(h, w_q, sub_keys, u_tab, v_tab):
    B, S, D = h.shape
    T = B * S
    xt = h.reshape(T, D)
    q = (xt @ w_q).reshape(T, PEER_HEADS, 2, PEER_HALF)
    s = jnp.einsum('thpd,hpkd->thpk', q, sub_keys).astype(F32)
    s_top, i_top = lax.top_k(s, PEER_TOPK)
    cand = (s_top[:, :, 0, :, None] + s_top[:, :, 1, None, :]).reshape(T, PEER_HEADS, -1)
    cand_idx = (i_top[:, :, 0, :, None] * PEER_NKEYS + i_top[:, :, 1, None, :]).reshape(T, PEER_HEADS, -1)
    best, pos = lax.top_k(cand, PEER_TOPK)
    idx = jnp.take_along_axis(cand_idx, pos, axis=-1)
    gates = jax.nn.softmax(best, axis=-1)

    def expert_block(args):
        xb, ib, gb = args
        act = jnp.einsum('chkd,cd->chk', u_tab[ib], xb)
        act = (jax.nn.gelu(act.astype(F32)) * gb).astype(xb.dtype)
        return jnp.einsum('chk,chkd->cd', act, v_tab[ib])

    nb = T // PEER_TOKEN_BLOCK
    out = lax.map(expert_block, (xt.reshape(nb, PEER_TOKEN_BLOCK, D),
                                 idx.reshape(nb, PEER_TOKEN_BLOCK, PEER_HEADS, PEER_TOPK),
                                 gates.reshape(nb, PEER_TOKEN_BLOCK, PEER_HEADS, PEER_TOPK)))
    return out.reshape(B, S, D)


def setup_inputs(seed: int = 0) -> dict:
    key = jax.random.key(seed)
    ks = iter(jax.random.split(key, 32))
    D = D_MODEL
    NA, NC, NS = N_ATTN_LAYERS, N_CONV_LAYERS, N_SGU_LAYERS
    DH = ATTN_HEAD_DIM

    def nrm(shape, scale):
        return jax.random.normal(next(ks), shape, F32) * scale

    return {
        "x": nrm((BATCH, SEQ, D), 1.0),
        "rel_bias": nrm((REL_BUCKETS, ATTN_HEADS), 0.5),
        "norm_mix": 1.0 + nrm((DEPTH, D), 0.02),
        "norm_ffn": 1.0 + nrm((DEPTH, D), 0.02),
        "attn_w_qkv": nrm((NA, D, 3 * D), D ** -0.5),
        "attn_q_gain": 1.0 + nrm((NA, DH), 0.02),
        "attn_k_gain": 1.0 + nrm((NA, DH), 0.02),
        "attn_lam_q1": nrm((NA, DH), 0.1),
        "attn_lam_k1": nrm((NA, DH), 0.1),
        "attn_lam_q2": nrm((NA, DH), 0.1),
        "attn_lam_k2": nrm((NA, DH), 0.1),
        "attn_sub_gain": 1.0 + nrm((NA, 2 * DH), 0.02),
        "attn_w_o": nrm((NA, D, D), D ** -0.5),
        "conv_w_in": nrm((NC, D, 3 * D), D ** -0.5),
        "conv_w": nrm((NC, CONV_WIDTH, D), CONV_WIDTH ** -0.5),
        "conv_w_out": nrm((NC, D, D), D ** -0.5),
        "sgu_w_in": nrm((NS, D, 2 * D), D ** -0.5),
        "sgu_v_gain": 1.0 + nrm((NS, D), 0.02),
        "sgu_w_s": nrm((NS, SGU_GROUPS, SGU_CHUNK, SGU_CHUNK), SGU_CHUNK ** -0.5),
        "sgu_b_s": 1.0 + nrm((NS, SGU_GROUPS, SGU_CHUNK), 0.02),
        "sgu_w_out": nrm((NS, D, D), D ** -0.5),
        "peer_w_q": nrm((DEPTH, D, PEER_HEADS * PEER_KEY_DIM), D ** -0.5),
        "peer_keys": nrm((DEPTH, PEER_HEADS, 2, PEER_NKEYS, PEER_HALF), PEER_HALF ** -0.5),
        "peer_u": nrm((DEPTH, PEER_EXPERTS, D), D ** -0.5),
        "peer_v": nrm((DEPTH, PEER_EXPERTS, D), 0.25),
    }


def reference(x, rel_bias, norm_mix, norm_ffn,
              attn_w_qkv, attn_q_gain, attn_k_gain, attn_lam_q1, attn_lam_k1, attn_lam_q2,
              attn_lam_k2, attn_sub_gain, attn_w_o,
              conv_w_in, conv_w, conv_w_out,
              sgu_w_in, sgu_v_gain, sgu_w_s, sgu_b_s, sgu_w_out,
              peer_w_q, peer_keys, peer_u, peer_v):
    h = x
    for i in range(DEPTH):
        m, j = i % N_MIXERS, i // N_MIXERS
        hn = rmsnorm(h, norm_mix[i])
        if m == 0:
            y = diff_attention(hn, attn_w_qkv[j], attn_q_gain[j], attn_k_gain[j],
                               attn_lam_q1[j], attn_lam_k1[j], attn_lam_q2[j], attn_lam_k2[j],
                               attn_sub_gain[j], attn_w_o[j], rel_bias, lambda_init_fn(i))
        elif m == 1:
            y = short_conv(hn, conv_w_in[j], conv_w[j], conv_w_out[j])
        else:
            y = spatial_gating_mlp(hn, sgu_w_in[j], sgu_v_gain[j], sgu_w_s[j], sgu_b_s[j],
                                   sgu_w_out[j])
        h = h + y
        h = h + peer_ffn(rmsnorm(h, norm_ffn[i]), peer_w_q[i], peer_keys[i], peer_u[i], peer_v[i])
    return h
```

```python
import functools
import math

import jax
import jax.numpy as jnp
from jax import lax
from jax.experimental import pallas as pl
from jax.experimental.pallas import tpu as pltpu

F32 = jnp.float32
BF16 = jnp.bfloat16

EPS = 1e-6
LANES = 128
V7X_VMEM_BYTES = 64 * 1024 * 1024
VMEM_LIMIT = V7X_VMEM_BYTES - 8 * 1024 * 1024

N_MIXERS = 3
ATTN_HEADS = 8
ATTN_HEAD_DIM = 64
REL_BUCKETS = 32
REL_MAX_DIST = 128
SGU_CHUNK = 128
SGU_GROUPS = 8
PEER_HEADS = 8
PEER_NKEYS = 128
PEER_HALF = 128
PEER_TOPK = 16

MASK_NEG = -1e30
KNOCKED = -3e38
RANK_NONE = 1000.0


def _params(*sem):
    return pltpu.CompilerParams(dimension_semantics=sem, vmem_limit_bytes=VMEM_LIMIT)


def _gelu(x):
    return 0.5 * x * (1.0 + jnp.tanh(0.7978845608028654 * (x + 0.044715 * (x * x * x))))


def _norm_matmul_kernel(x_ref, g_ref, w_ref, o_ref, xn_ref, *, act):
    @pl.when(pl.program_id(1) == 0)
    def _():
        x = x_ref[...]
        ms = jnp.mean(x * x, axis=-1, keepdims=True)
        xn_ref[...] = (x * lax.rsqrt(ms + EPS) * g_ref[...]).astype(BF16)

    y = jnp.dot(xn_ref[...], w_ref[...], preferred_element_type=F32)
    if act == "gelu":
        y = _gelu(y)
    o_ref[...] = y.astype(o_ref.dtype)


def norm_matmul(x, g, w, *, act=None, out_dtype=F32, tm=1024, tn=1024):
    m, k = x.shape
    n = w.shape[1]
    tm, tn = min(tm, m), min(tn, n)
    return pl.pallas_call(
        functools.partial(_norm_matmul_kernel, act=act),
        grid=(m // tm, n // tn),
        in_specs=[pl.BlockSpec((tm, k), lambda i, j: (i, 0)),
                  pl.BlockSpec((1, k), lambda i, j: (0, 0)),
                  pl.BlockSpec((k, tn), lambda i, j: (0, j))],
        out_specs=pl.BlockSpec((tm, tn), lambda i, j: (i, j)),
        out_shape=jax.ShapeDtypeStruct((m, n), out_dtype),
        scratch_shapes=[pltpu.VMEM((tm, k), BF16)],
        compiler_params=_params("parallel", "arbitrary"),
        name="norm_matmul",
    )(x, g.reshape(1, k), w)


def _matmul_residual_kernel(a_ref, w_ref, r_ref, o_ref):
    o_ref[...] = r_ref[...] + jnp.dot(a_ref[...], w_ref[...], preferred_element_type=F32)


def matmul_residual(a, w, res, *, tm=1024):
    m, k = a.shape
    n = w.shape[1]
    tm = min(tm, m)
    return pl.pallas_call(
        _matmul_residual_kernel,
        grid=(m // tm,),
        in_specs=[pl.BlockSpec((tm, k), lambda i: (i, 0)),
                  pl.BlockSpec((k, n), lambda i: (0, 0)),
                  pl.BlockSpec((tm, n), lambda i: (i, 0))],
        out_specs=pl.BlockSpec((tm, n), lambda i: (i, 0)),
        out_shape=jax.ShapeDtypeStruct((m, n), F32),
        compiler_params=_params("parallel"),
        name="matmul_residual",
    )(a, w, res)


def _matmul_kernel(a_ref, b_ref, o_ref):
    o_ref[...] = jnp.dot(a_ref[...], b_ref[...], preferred_element_type=F32).astype(o_ref.dtype)


def matmul(a, b, *, out_dtype=F32, tm=1024, tn=1024):
    m, k = a.shape
    n = b.shape[1]
    tm, tn = min(tm, m), min(tn, n)
    return pl.pallas_call(
        _matmul_kernel,
        grid=(n // tn, m // tm),
        in_specs=[pl.BlockSpec((tm, k), lambda j, i: (i, 0)),
                  pl.BlockSpec((k, tn), lambda j, i: (0, j))],
        out_specs=pl.BlockSpec((tm, tn), lambda j, i: (i, j)),
        out_shape=jax.ShapeDtypeStruct((m, n), out_dtype),
        compiler_params=_params("parallel", "arbitrary"),
        name="matmul",
    )(a, b)


def _qk_norm_kernel(q_ref, k_ref, v_ref, qg_ref, kg_ref, qo_ref, ko_ref, vo_ref):
    lane = lax.broadcasted_iota(jnp.int32, (1, LANES), 1)
    low = lane < ATTN_HEAD_DIM

    def group_norm(x, gain):
        sq = x * x
        tot = jnp.sum(sq, axis=-1, keepdims=True)
        lo = jnp.sum(jnp.where(low, sq, 0.0), axis=-1, keepdims=True)
        ms = jnp.where(low, lo, tot - lo) * (1.0 / ATTN_HEAD_DIM)
        return x * lax.rsqrt(ms + EPS) * gain

    n_slabs = q_ref.shape[1] // LANES
    for c in range(n_slabs):
        sl = slice(c * LANES, (c + 1) * LANES)
        qo_ref[:, sl] = group_norm(q_ref[:, sl], qg_ref[...]).astype(BF16)
        ko_ref[:, sl] = group_norm(k_ref[:, sl], kg_ref[...]).astype(BF16)
    vo_ref[...] = v_ref[...].astype(BF16)


def qk_norm(qkv, q_gain, k_gain, *, tm=512):
    t, d3 = qkv.shape
    d = d3 // 3
    tm = min(tm, t)
    scale = ATTN_HEAD_DIM ** -0.5
    qg = jnp.tile(q_gain.astype(F32) * scale, 2).reshape(1, LANES)
    kg = jnp.tile(k_gain.astype(F32), 2).reshape(1, LANES)
    out = jax.ShapeDtypeStruct((t, d), BF16)
    return pl.pallas_call(
        _qk_norm_kernel,
        grid=(t // tm,),
        in_specs=[pl.BlockSpec((tm, d), lambda i: (i, 0)),
                  pl.BlockSpec((tm, d), lambda i: (i, 1)),
                  pl.BlockSpec((tm, d), lambda i: (i, 2)),
                  pl.BlockSpec((1, LANES), lambda i: (0, 0)),
                  pl.BlockSpec((1, LANES), lambda i: (0, 0))],
        out_specs=[pl.BlockSpec((tm, d), lambda i: (i, 0))] * 3,
        out_shape=[out, out, out],
        compiler_params=_params("parallel"),
        name="qk_norm",
    )(qkv, qkv, qkv, qg, kg)


def _attn_kernel(lam_ref, q_ref, k_ref, v_ref, bias_ref, sg_ref, o_ref,
                 acc0, acc1, m0, l0, m1, l1, *, tb, out_scale):
    qi = pl.program_id(2)
    lane = lax.broadcasted_iota(jnp.int32, (1, LANES), 1)
    q = q_ref[0]
    zero = jnp.zeros_like(q)
    qs = (jnp.where(lane < ATTN_HEAD_DIM, q, zero), jnp.where(lane >= ATTN_HEAD_DIM, q, zero))
    state = ((acc0, m0, l0), (acc1, m1, l1))
    for acc, m, l in state:
        acc[...] = jnp.zeros_like(acc)
        m[...] = jnp.full_like(m, MASK_NEG)
        l[...] = jnp.zeros_like(l)

    def body(ki, carry):
        off = pl.multiple_of(ki * tb, tb)
        k = k_ref[0, pl.ds(off, tb), :]
        v = v_ref[0, pl.ds(off, tb), :]
        bias = bias_ref[0, jnp.minimum(qi - ki, 2)]
        for qm, (acc, m, l) in zip(qs, state):
            s = lax.dot_general(qm, k, (((1,), (1,)), ((), ())), preferred_element_type=F32) + bias
            m_old = m[...]
            m_new = jnp.maximum(m_old, jnp.max(s, axis=-1, keepdims=True))
            alpha = jnp.exp(m_old - m_new)
            p = jnp.exp(s - m_new)
            l[...] = alpha * l[...] + jnp.sum(p, axis=-1, keepdims=True)
            acc[...] = alpha * acc[...] + jnp.dot(p.astype(BF16), v, preferred_element_type=F32)
            m[...] = m_new
        return carry

    lax.fori_loop(0, qi + 1, body, 0)
    lam = lam_ref[0]
    o = acc0[...] / l0[...] - lam * (acc1[...] / l1[...])
    ms = jnp.mean(o * o, axis=-1, keepdims=True)
    o_ref[0] = (o * lax.rsqrt(ms + EPS) * (sg_ref[...] * out_scale)).astype(o_ref.dtype)


def _rel_bucket(dist):
    n = jnp.maximum(dist, 0)
    max_exact = REL_BUCKETS // 2
    nf = jnp.maximum(n, 1).astype(F32)
    large = max_exact + (jnp.log(nf / max_exact) / math.log(REL_MAX_DIST / max_exact)
                         * (REL_BUCKETS - max_exact)).astype(jnp.int32)
    large = jnp.minimum(large, REL_BUCKETS - 1)
    return jnp.where(n < max_exact, n, large)


def _bias_tables(rel_bias, tb):
    assert tb >= REL_MAX_DIST
    r = jnp.arange(tb, dtype=jnp.int32)
    d0 = r[:, None] - r[None, :]
    dist = jnp.stack([d0, d0 + tb, d0 + 2 * tb])
    tab = jnp.transpose(rel_bias.astype(F32)[_rel_bucket(dist)], (3, 0, 1, 2))
    return jnp.where((dist >= 0)[None], tab, MASK_NEG)


def diff_attention_core(q, k, v, rel_bias, lam, sub_gain, lam_init, *, tb=512):
    b, s, d = q.shape
    tb = min(tb, s)
    hd = 2 * ATTN_HEAD_DIM
    bias = _bias_tables(rel_bias, tb)
    return pl.pallas_call(
        functools.partial(_attn_kernel, tb=tb, out_scale=1.0 - lam_init),
        grid=(b, ATTN_HEADS, s // tb),
        in_specs=[pl.BlockSpec(memory_space=pltpu.SMEM),
                  pl.BlockSpec((1, tb, hd), lambda bi, h, qi: (bi, qi, h)),
                  pl.BlockSpec((1, s, hd), lambda bi, h, qi: (bi, 0, h)),
                  pl.BlockSpec((1, s, hd), lambda bi, h, qi: (bi, 0, h)),
                  pl.BlockSpec((1, 3, tb, tb), lambda bi, h, qi: (h, 0, 0, 0)),
                  pl.BlockSpec((1, hd), lambda bi, h, qi: (0, 0))],
        out_specs=pl.BlockSpec((1, tb, hd), lambda bi, h, qi: (bi, qi, h)),
        out_shape=jax.ShapeDtypeStruct((b, s, d), BF16),
        scratch_shapes=[pltpu.VMEM((tb, hd), F32), pltpu.VMEM((tb, hd), F32),
                        pltpu.VMEM((tb, 1), F32), pltpu.VMEM((tb, 1), F32),
                        pltpu.VMEM((tb, 1), F32), pltpu.VMEM((tb, 1), F32)],
        compiler_params=_params("parallel", "parallel", "arbitrary"),
        name="diff_attention",
    )(lam.reshape(1).astype(F32), q, k, v, bias, sub_gain.astype(F32).reshape(1, hd))


def _conv_gate_kernel(b_ref, c_ref, x_ref, w_ref, o_ref):
    z = c_ref[0] * x_ref[0]
    row = lax.broadcasted_iota(jnp.int32, z.shape, 0)
    z1 = jnp.where(row >= 1, pltpu.roll(z, 1, 0), 0.0)
    z2 = jnp.where(row >= 2, pltpu.roll(z, 2, 0), 0.0)
    zc = w_ref[0:1, :] * z2 + w_ref[1:2, :] * z1 + w_ref[2:3, :] * z
    o_ref[0] = (b_ref[0] * zc).astype(o_ref.dtype)


def conv_gate(bcx, conv_w):
    b, s, d3 = bcx.shape
    d = d3 // 3
    nd = d // LANES
    return pl.pallas_call(
        _conv_gate_kernel,
        grid=(b, nd),
        in_specs=[pl.BlockSpec((1, s, LANES), lambda bi, j: (bi, 0, j)),
                  pl.BlockSpec((1, s, LANES), lambda bi, j: (bi, 0, nd + j)),
                  pl.BlockSpec((1, s, LANES), lambda bi, j: (bi, 0, 2 * nd + j)),
                  pl.BlockSpec((3, LANES), lambda bi, j: (0, j))],
        out_specs=pl.BlockSpec((1, s, LANES), lambda bi, j: (bi, 0, j)),
        out_shape=jax.ShapeDtypeStruct((b, s, d), BF16),
        compiler_params=_params("parallel", "parallel"),
        name="conv_gate",
    )(bcx, bcx, bcx, conv_w.astype(F32))


def _sgu_gate_kernel(z_ref, vg_ref, ws_ref, bs_ref, o_ref):
    d = o_ref.shape[1]
    v = z_ref[:, d:]
    ms = jnp.mean(v * v, axis=-1, keepdims=True)
    vn = (v * lax.rsqrt(ms + EPS) * vg_ref[...]).astype(BF16)
    row = lax.broadcasted_iota(jnp.int32, (SGU_CHUNK, SGU_CHUNK), 0)
    col = lax.broadcasted_iota(jnp.int32, (SGU_CHUNK, SGU_CHUNK), 1)
    for g in range(SGU_GROUPS):
        sl = slice(g * LANES, (g + 1) * LANES)
        w = jnp.where(row >= col, ws_ref[g], 0.0).astype(BF16)
        sv = jnp.dot(w, vn[:, sl], preferred_element_type=F32) + bs_ref[:, g:g + 1]
        o_ref[:, sl] = (z_ref[:, sl] * sv).astype(o_ref.dtype)


def sgu_gate(z, v_gain, w_s, b_s):
    t, d2 = z.shape
    d = d2 // 2
    return pl.pallas_call(
        _sgu_gate_kernel,
        grid=(t // SGU_CHUNK,),
        in_specs=[pl.BlockSpec((SGU_CHUNK, d2), lambda i: (i, 0)),
                  pl.BlockSpec((1, d), lambda i: (0, 0)),
                  pl.BlockSpec((SGU_GROUPS, SGU_CHUNK, SGU_CHUNK), lambda i: (0, 0, 0)),
                  pl.BlockSpec((SGU_CHUNK, SGU_GROUPS), lambda i: (0, 0))],
        out_specs=pl.BlockSpec((SGU_CHUNK, d), lambda i: (i, 0)),
        out_shape=jax.ShapeDtypeStruct((t, d), BF16),
        compiler_params=_params("parallel"),
        name="sgu_gate",
    )(z, v_gain.astype(F32).reshape(1, d), w_s.astype(F32), jnp.transpose(b_s.astype(F32)))


def _norm_transpose_kernel(x_ref, g_ref, o_ref):
    x = x_ref[...]
    ms = jnp.mean(x * x, axis=-1, keepdims=True)
    o_ref[...] = jnp.transpose(x * lax.rsqrt(ms + EPS) * g_ref[...]).astype(o_ref.dtype)


def norm_transpose(x, g, *, tm=512):
    m, k = x.shape
    tm = min(tm, m)
    return pl.pallas_call(
        _norm_transpose_kernel,
        grid=(m // tm,),
        in_specs=[pl.BlockSpec((tm, k), lambda i: (i, 0)),
                  pl.BlockSpec((1, k), lambda i: (0, 0))],
        out_specs=pl.BlockSpec((k, tm), lambda i: (0, i)),
        out_shape=jax.ShapeDtypeStruct((k, m), BF16),
        compiler_params=_params("parallel"),
        name="norm_transpose",
    )(x, g.astype(F32).reshape(1, k))


_A_LAST_K1 = tuple(PEER_TOPK // (k2 + 1) - 1 for k2 in range(8))


def _top16(s, t_ref):
    rank = jnp.full(s.shape, RANK_NONE, F32)
    for kk in range(PEER_TOPK):
        mx = jnp.max(s, axis=0, keepdims=True)
        hit = s == mx
        rank = jnp.where(hit, float(kk), rank)
        s = jnp.where(hit, KNOCKED, s)
        t_ref[kk:kk + 1, :] = mx
    return rank


def _peer_select_kernel(q_ref, keys_ref, r2_ref, n1_ref, a_ref, b_ref, t1_ref, t2_ref):
    row8 = lax.broadcasted_iota(jnp.int32, (8, LANES), 0)

    def head(h, carry):
        base = pl.multiple_of(h * (2 * PEER_HALF), 2 * PEER_HALF)
        q1 = q_ref[pl.ds(base, PEER_HALF), :].astype(BF16)
        q2 = q_ref[pl.ds(base + PEER_HALF, PEER_HALF), :].astype(BF16)
        s1 = jnp.dot(keys_ref[h, 0], q1, preferred_element_type=F32)
        s2 = jnp.dot(keys_ref[h, 1], q2, preferred_element_type=F32)
        rank1 = _top16(s1, t1_ref)
        rank2 = _top16(s2, t2_ref)
        t1 = t1_ref[...]
        t2 = t2_ref[...]
        t1_lo, t1_hi = t1[0:8], t1[8:16]

        cand = [jnp.where(row8 <= _A_LAST_K1[k2], t1_lo + t2[k2:k2 + 1], KNOCKED) for k2 in range(8)]
        cand.append(t1_hi + t2[0:1])
        cand.append(t1[0:1] + t2[8:16])
        work = list(cand)
        thr = None
        for _ in range(PEER_TOPK):
            mx = functools.reduce(jnp.maximum, work)
            thr = jnp.max(mx, axis=0, keepdims=True)
            work = [jnp.where(w == thr, KNOCKED, w) for w in work]
        top = t1[0:1] + t2[0:1]

        sel = [c >= thr for c in cand]
        ex = [jnp.where(sl, jnp.exp(c - top), 0.0) for sl, c in zip(sel, cand)]
        z = jnp.sum(functools.reduce(jnp.add, ex), axis=0, keepdims=True)
        cnt = [jnp.where(sl, 1.0, 0.0) for sl in sel]
        n_lo = functools.reduce(jnp.add, cnt[:8])
        n_lo = n_lo + jnp.where(row8 == 0, jnp.sum(cnt[9], axis=0, keepdims=True), 0.0)
        n_hi = cnt[8]

        n1 = jnp.zeros(rank1.shape, F32)
        for kk in range(PEER_TOPK):
            src = n_lo if kk < 8 else n_hi
            n1 = jnp.where(rank1 == float(kk), src[kk % 8:kk % 8 + 1], n1)

        r2_ref[h] = rank2
        n1_ref[h] = n1
        a_ref[h] = jnp.exp(s1 - t1[0:1])
        b_ref[h] = jnp.exp(s2 - t2[0:1]) / z
        return carry

    lax.fori_loop(0, PEER_HEADS, head, 0)


def peer_select(q_t, keys):
    hq, t = q_t.shape
    out = jax.ShapeDtypeStruct((PEER_HEADS, PEER_NKEYS, t), F32)
    ospec = pl.BlockSpec((PEER_HEADS, PEER_NKEYS, LANES), lambda i: (0, 0, i))
    return pl.pallas_call(
        _peer_select_kernel,
        grid=(t // LANES,),
        in_specs=[pl.BlockSpec((hq, LANES), lambda i: (0, i)),
                  pl.BlockSpec((PEER_HEADS, 2, PEER_NKEYS, PEER_HALF), lambda i: (0, 0, 0, 0))],
        out_specs=[ospec] * 4,
        out_shape=[out] * 4,
        scratch_shapes=[pltpu.VMEM((PEER_TOPK, LANES), F32), pltpu.VMEM((PEER_TOPK, LANES), F32)],
        compiler_params=_params("parallel"),
        name="peer_select",
    )(q_t, keys)


def _peer_dense_kernel(x_ref, u_ref, vt_ref, r2_ref, b_ref, n1_ref, a_ref, res_ref, o_ref,
                       acc_ref, at_ref, y_ref):
    j = pl.program_id(1)

    @pl.when(j == 0)
    def _():
        acc_ref[...] = jnp.zeros_like(acc_ref)

    at_ref[...] = jnp.dot(u_ref[...], x_ref[...], preferred_element_type=F32)
    n_rows = at_ref.shape[0] // PEER_NKEYS
    for lt in range(at_ref.shape[1] // LANES):
        ls = slice(lt * LANES, (lt + 1) * LANES)
        for r in range(n_rows):
            rs = slice(r * PEER_NKEYS, (r + 1) * PEER_NKEYS)
            w = jnp.zeros((PEER_NKEYS, LANES), F32)
            for h in range(PEER_HEADS):
                picked = r2_ref[h, :, ls] < n1_ref[h, r:r + 1, ls]
                w = w + jnp.where(picked, b_ref[h, :, ls], 0.0) * a_ref[h, r:r + 1, ls]
            y_ref[rs, ls] = (_gelu(at_ref[rs, ls]) * w).astype(BF16)
    acc_ref[...] += jnp.dot(vt_ref[...], y_ref[...], preferred_element_type=F32)

    @pl.when(j == pl.num_programs(1) - 1)
    def _():
        o_ref[...] = res_ref[...] + jnp.transpose(acc_ref[...])


def peer_dense(xn_t, u, v_t, r2, b, n1, a, res, *, tt=512, ec=1024):
    d, t = xn_t.shape
    e = u.shape[0]
    tt = min(tt, t)
    rows = ec // PEER_NKEYS
    sel_full = pl.BlockSpec((PEER_HEADS, PEER_NKEYS, tt), lambda i, j: (0, 0, i))
    sel_rows = pl.BlockSpec((PEER_HEADS, rows, tt), lambda i, j: (0, j, i))
    return pl.pallas_call(
        _peer_dense_kernel,
        grid=(t // tt, e // ec),
        in_specs=[pl.BlockSpec((d, tt), lambda i, j: (0, i)),
                  pl.BlockSpec((ec, d), lambda i, j: (j, 0)),
                  pl.BlockSpec((d, ec), lambda i, j: (0, j)),
                  sel_full, sel_full, sel_rows, sel_rows,
                  pl.BlockSpec((tt, d), lambda i, j: (i, 0))],
        out_specs=pl.BlockSpec((tt, d), lambda i, j: (i, 0)),
        out_shape=jax.ShapeDtypeStruct((t, d), F32),
        scratch_shapes=[pltpu.VMEM((d, tt), F32), pltpu.VMEM((ec, tt), F32),
                        pltpu.VMEM((ec, tt), BF16)],
        compiler_params=_params("parallel", "arbitrary"),
        name="peer_dense",
    )(xn_t, u, v_t, r2, b, n1, a, res)


def peer_ffn(h2, g, w_q, keys, u, v):
    xn_t = norm_transpose(h2, g)
    q_t = matmul(jnp.transpose(w_q).astype(BF16), xn_t)
    r2, n1, a, b = peer_select(q_t, keys.astype(BF16))
    return peer_dense(xn_t, u.astype(BF16), jnp.transpose(v).astype(BF16), r2, b, n1, a, h2)


def _lambda_init(layer_idx):
    return 0.8 - 0.6 * math.exp(-0.3 * layer_idx)


def kernel(x, rel_bias, norm_mix, norm_ffn, attn_w_qkv, attn_q_gain, attn_k_gain, attn_lam_q1,
           attn_lam_k1, attn_lam_q2, attn_lam_k2, attn_sub_gain, attn_w_o, conv_w_in, conv_w,
           conv_w_out, sgu_w_in, sgu_v_gain, sgu_w_s, sgu_b_s, sgu_w_out, peer_w_q, peer_keys,
           peer_u, peer_v):
    bsz, seq, d = x.shape
    t = bsz * seq
    h = x.reshape(t, d).astype(F32)
    depth = norm_mix.shape[0]
    for i in range(depth):
        m, j = i % N_MIXERS, i // N_MIXERS
        g = norm_mix[i].astype(F32)
        if m == 0:
            qkv = norm_matmul(h, g, attn_w_qkv[j].astype(BF16))
            q, k, v = qk_norm(qkv, attn_q_gain[j], attn_k_gain[j])
            lam_init = _lambda_init(i)
            lam = (jnp.exp(jnp.sum(attn_lam_q1[j].astype(F32) * attn_lam_k1[j].astype(F32)))
                   - jnp.exp(jnp.sum(attn_lam_q2[j].astype(F32) * attn_lam_k2[j].astype(F32)))
                   + lam_init)
            o = diff_attention_core(q.reshape(bsz, seq, d), k.reshape(bsz, seq, d),
                                    v.reshape(bsz, seq, d), rel_bias, lam, attn_sub_gain[j],
                                    lam_init)
            h = matmul_residual(o.reshape(t, d), attn_w_o[j].astype(BF16), h)
        elif m == 1:
            bcx = norm_matmul(h, g, conv_w_in[j].astype(BF16))
            gated = conv_gate(bcx.reshape(bsz, seq, 3 * d), conv_w[j])
            h = matmul_residual(gated.reshape(t, d), conv_w_out[j].astype(BF16), h)
        else:
            z = norm_matmul(h, g, sgu_w_in[j].astype(BF16), act="gelu")
            gated = sgu_gate(z, sgu_v_gain[j], sgu_w_s[j], sgu_b_s[j])
            h = matmul_residual(gated, sgu_w_out[j].astype(BF16), h)
        h = peer_ffn(h, norm_ffn[i], peer_w_q[i], peer_keys[i], peer_u[i], peer_v[i])
    return h.reshape(bsz, seq, d).astype(x.dtype)
```

```python
import functools
import math

import numpy as np
import jax
import jax.numpy as jnp
from jax import lax
from jax.experimental import pallas as pl
from jax.experimental.pallas import tpu as pltpu

F32 = jnp.float32
BF16 = jnp.bfloat16

EPS = 1e-6
LANES = 128
BF16_ROWS = 16
MXU_ROWS = 256
V7X_VMEM_BYTES = 64 * 1024 * 1024
VMEM_LIMIT = V7X_VMEM_BYTES - 8 * 1024 * 1024

N_MIXERS = 3
ATTN_HEADS = 8
ATTN_HEAD_DIM = 64
ATTN_BLOCK = 512
REL_BUCKETS = 32
REL_MAX_DIST = 128
SGU_CHUNK = 128
SGU_GROUPS = 8
PEER_HEADS = 8
PEER_NKEYS = 128
PEER_HALF = 128
PEER_TOPK = 16

MASK_NEG = -1e30
KNOCKED = -3e38
RANK_NONE = 256.0


def _params(*sem):
    return pltpu.CompilerParams(dimension_semantics=sem, vmem_limit_bytes=VMEM_LIMIT)


def _gelu(x):
    return 0.5 * x * (1.0 + jnp.tanh(0.7978845608028654 * (x + 0.044715 * (x * x * x))))


def _norm_matmul_kernel(x_ref, g_ref, w_ref, o_ref, xn_ref, *, act):
    @pl.when(pl.program_id(1) == 0)
    def _():
        x = x_ref[...]
        ms = jnp.mean(x * x, axis=-1, keepdims=True)
        xn_ref[...] = (x * lax.rsqrt(ms + EPS) * g_ref[...]).astype(BF16)

    y = jnp.dot(xn_ref[...], w_ref[...], preferred_element_type=F32)
    if act == "gelu":
        y = _gelu(y)
    o_ref[...] = y.astype(o_ref.dtype)


def norm_matmul(x, g, w, *, act=None, out_dtype=F32, tm=1024, tn=1024):
    m, k = x.shape
    n = w.shape[1]
    tm, tn = min(tm, m), min(tn, n)
    return pl.pallas_call(
        functools.partial(_norm_matmul_kernel, act=act),
        grid=(m // tm, n // tn),
        in_specs=[pl.BlockSpec((tm, k), lambda i, j: (i, 0)),
                  pl.BlockSpec((1, k), lambda i, j: (0, 0)),
                  pl.BlockSpec((k, tn), lambda i, j: (0, j))],
        out_specs=pl.BlockSpec((tm, tn), lambda i, j: (i, j)),
        out_shape=jax.ShapeDtypeStruct((m, n), out_dtype),
        scratch_shapes=[pltpu.VMEM((tm, k), BF16)],
        compiler_params=_params("parallel", "arbitrary"),
        name="norm_matmul",
    )(x, g.reshape(1, k), w)


def _matmul_residual_kernel(a_ref, w_ref, r_ref, o_ref):
    o_ref[...] = r_ref[...] + jnp.dot(a_ref[...], w_ref[...], preferred_element_type=F32)


def matmul_residual(a, w, res, *, tm=1024):
    m, k = a.shape
    n = w.shape[1]
    tm = min(tm, m)
    return pl.pallas_call(
        _matmul_residual_kernel,
        grid=(m // tm,),
        in_specs=[pl.BlockSpec((tm, k), lambda i: (i, 0)),
                  pl.BlockSpec((k, n), lambda i: (0, 0)),
                  pl.BlockSpec((tm, n), lambda i: (i, 0))],
        out_specs=pl.BlockSpec((tm, n), lambda i: (i, 0)),
        out_shape=jax.ShapeDtypeStruct((m, n), F32),
        compiler_params=_params("parallel"),
        name="matmul_residual",
    )(a, w, res)


def _matmul_kernel(a_ref, b_ref, o_ref):
    o_ref[...] = jnp.dot(a_ref[...], b_ref[...], preferred_element_type=F32).astype(o_ref.dtype)


def matmul(a, b, *, out_dtype=F32, tm=1024, tn=1024):
    m, k = a.shape
    n = b.shape[1]
    tm, tn = min(tm, m), min(tn, n)
    return pl.pallas_call(
        _matmul_kernel,
        grid=(n // tn, m // tm),
        in_specs=[pl.BlockSpec((tm, k), lambda j, i: (i, 0)),
                  pl.BlockSpec((k, tn), lambda j, i: (0, j))],
        out_specs=pl.BlockSpec((tm, tn), lambda j, i: (i, j)),
        out_shape=jax.ShapeDtypeStruct((m, n), out_dtype),
        compiler_params=_params("parallel", "arbitrary"),
        name="matmul",
    )(a, b)


def _qk_norm_kernel(q_ref, k_ref, v_ref, qg_ref, kg_ref, qo_ref, ko_ref, vo_ref):
    lane = lax.broadcasted_iota(jnp.int32, (1, LANES), 1)
    low = lane < ATTN_HEAD_DIM

    def group_norm(x, gain):
        sq = x * x
        tot = jnp.sum(sq, axis=-1, keepdims=True)
        lo = jnp.sum(jnp.where(low, sq, 0.0), axis=-1, keepdims=True)
        ms = jnp.where(low, lo, tot - lo) * (1.0 / ATTN_HEAD_DIM)
        return x * lax.rsqrt(ms + EPS) * gain

    n_slabs = q_ref.shape[1] // LANES
    for c in range(n_slabs):
        sl = slice(c * LANES, (c + 1) * LANES)
        qo_ref[:, sl] = group_norm(q_ref[:, sl], qg_ref[...]).astype(BF16)
        ko_ref[:, sl] = group_norm(k_ref[:, sl], kg_ref[...]).astype(BF16)
    vo_ref[...] = v_ref[...].astype(BF16)


def qk_norm(qkv, q_gain, k_gain, *, tm=512):
    t, d3 = qkv.shape
    d = d3 // 3
    tm = min(tm, t)
    scale = ATTN_HEAD_DIM ** -0.5
    qg = jnp.tile(q_gain.astype(F32) * scale, 2).reshape(1, LANES)
    kg = jnp.tile(k_gain.astype(F32), 2).reshape(1, LANES)
    out = jax.ShapeDtypeStruct((t, d), BF16)
    return pl.pallas_call(
        _qk_norm_kernel,
        grid=(t // tm,),
        in_specs=[pl.BlockSpec((tm, d), lambda i: (i, 0)),
                  pl.BlockSpec((tm, d), lambda i: (i, 1)),
                  pl.BlockSpec((tm, d), lambda i: (i, 2)),
                  pl.BlockSpec((1, LANES), lambda i: (0, 0)),
                  pl.BlockSpec((1, LANES), lambda i: (0, 0))],
        out_specs=[pl.BlockSpec((tm, d), lambda i: (i, 0))] * 3,
        out_shape=[out, out, out],
        compiler_params=_params("parallel"),
        name="qk_norm",
    )(qkv, qkv, qkv, qg, kg)


def _attn_kernel(lam_ref, q_ref, k_ref, v_ref, bias_ref, sg_ref, o_ref,
                 acc0, acc1, m0, l0, m1, l1, *, tb, out_scale):
    qi = pl.program_id(2)
    lane = lax.broadcasted_iota(jnp.int32, (1, LANES), 1)
    q = q_ref[0]
    zero = jnp.zeros_like(q)
    qs = (jnp.where(lane < ATTN_HEAD_DIM, q, zero), jnp.where(lane >= ATTN_HEAD_DIM, q, zero))
    state = ((acc0, m0, l0), (acc1, m1, l1))
    for acc, m, l in state:
        acc[...] = jnp.zeros_like(acc)
        m[...] = jnp.full_like(m, MASK_NEG)
        l[...] = jnp.zeros_like(l)

    def body(ki, carry):
        off = pl.multiple_of(ki * tb, tb)
        k = k_ref[0, pl.ds(off, tb), :]
        v = v_ref[0, pl.ds(off, tb), :]
        bias = bias_ref[0, jnp.minimum(qi - ki, 2)]
        for qm, (acc, m, l) in zip(qs, state):
            s = lax.dot_general(qm, k, (((1,), (1,)), ((), ())), preferred_element_type=F32) + bias
            m_old = m[...]
            m_new = jnp.maximum(m_old, jnp.max(s, axis=-1, keepdims=True))
            alpha = jnp.exp(m_old - m_new)
            p = jnp.exp(s - m_new)
            l[...] = alpha * l[...] + jnp.sum(p, axis=-1, keepdims=True)
            acc[...] = alpha * acc[...] + jnp.dot(p.astype(BF16), v, preferred_element_type=F32)
            m[...] = m_new
        return carry

    lax.fori_loop(0, qi + 1, body, 0)
    lam = lam_ref[0]
    o = acc0[...] / l0[...] - lam * (acc1[...] / l1[...])
    ms = jnp.mean(o * o, axis=-1, keepdims=True)
    o_ref[0] = (o * lax.rsqrt(ms + EPS) * (sg_ref[...] * out_scale)).astype(o_ref.dtype)


def _bucket_thresholds(max_dist):
    n = np.arange(max_dist, dtype=np.int32)
    max_exact = REL_BUCKETS // 2
    nf = np.maximum(n, 1).astype(np.float32)
    large = max_exact + (np.log(nf / np.float32(max_exact))
                         / np.float32(math.log(REL_MAX_DIST / max_exact))
                         * np.float32(REL_BUCKETS - max_exact)).astype(np.int32)
    bucket = np.where(n < max_exact, n, np.minimum(large, REL_BUCKETS - 1))
    assert bucket[-1] == REL_BUCKETS - 1 and np.all(np.diff(bucket) >= 0)
    return tuple(int(np.argmax(bucket >= k)) for k in range(1, REL_BUCKETS))


def _bias_tables_kernel(rb_ref, o_ref, *, tb, thresholds):
    h = pl.program_id(0)
    row = lax.broadcasted_iota(jnp.int32, (tb, tb), 0)
    col = lax.broadcasted_iota(jnp.int32, (tb, tb), 1)
    for j in range(2):
        dist = row - col + j * tb
        val = jnp.full((tb, tb), rb_ref[0, h], F32)
        for k, thr in enumerate(thresholds, start=1):
            val = jnp.where(dist >= thr, rb_ref[k, h], val)
        o_ref[0, j] = jnp.where(dist >= 0, val, MASK_NEG)
    o_ref[0, 2] = jnp.full((tb, tb), rb_ref[REL_BUCKETS - 1, h], F32)


def _bias_tables(rel_bias, tb):
    thresholds = _bucket_thresholds(2 * tb)
    assert thresholds[-1] <= tb + 1
    return pl.pallas_call(
        functools.partial(_bias_tables_kernel, tb=tb, thresholds=thresholds),
        grid=(ATTN_HEADS,),
        in_specs=[pl.BlockSpec(memory_space=pltpu.SMEM)],
        out_specs=pl.BlockSpec((1, 3, tb, tb), lambda h: (h, 0, 0, 0)),
        out_shape=jax.ShapeDtypeStruct((ATTN_HEADS, 3, tb, tb), F32),
        compiler_params=_params("parallel"),
        name="rel_bias_tables",
    )(rel_bias.astype(F32))


def diff_attention_core(q, k, v, bias, lam, sub_gain, lam_init):
    b, s, d = q.shape
    tb = bias.shape[-1]
    hd = 2 * ATTN_HEAD_DIM
    return pl.pallas_call(
        functools.partial(_attn_kernel, tb=tb, out_scale=1.0 - lam_init),
        grid=(b, ATTN_HEADS, s // tb),
        in_specs=[pl.BlockSpec(memory_space=pltpu.SMEM),
                  pl.BlockSpec((1, tb, hd), lambda bi, h, qi: (bi, qi, h)),
                  pl.BlockSpec((1, s, hd), lambda bi, h, qi: (bi, 0, h)),
                  pl.BlockSpec((1, s, hd), lambda bi, h, qi: (bi, 0, h)),
                  pl.BlockSpec((1, 3, tb, tb), lambda bi, h, qi: (h, 0, 0, 0)),
                  pl.BlockSpec((1, hd), lambda bi, h, qi: (0, 0))],
        out_specs=pl.BlockSpec((1, tb, hd), lambda bi, h, qi: (bi, qi, h)),
        out_shape=jax.ShapeDtypeStruct((b, s, d), BF16),
        scratch_shapes=[pltpu.VMEM((tb, hd), F32), pltpu.VMEM((tb, hd), F32),
                        pltpu.VMEM((tb, 1), F32), pltpu.VMEM((tb, 1), F32),
                        pltpu.VMEM((tb, 1), F32), pltpu.VMEM((tb, 1), F32)],
        compiler_params=_params("parallel", "parallel", "arbitrary"),
        name="diff_attention",
    )(lam.reshape(1).astype(F32), q, k, v, bias, sub_gain.astype(F32).reshape(1, hd))


def _conv_gate_kernel(b_ref, c_ref, x_ref, w_ref, o_ref):
    z = c_ref[0] * x_ref[0]
    row = lax.broadcasted_iota(jnp.int32, z.shape, 0)
    z1 = jnp.where(row >= 1, pltpu.roll(z, 1, 0), 0.0)
    z2 = jnp.where(row >= 2, pltpu.roll(z, 2, 0), 0.0)
    zc = w_ref[0:1, :] * z2 + w_ref[1:2, :] * z1 + w_ref[2:3, :] * z
    o_ref[0] = (b_ref[0] * zc).astype(o_ref.dtype)


def conv_gate(bcx, conv_w):
    b, s, d3 = bcx.shape
    d = d3 // 3
    nd = d // LANES
    return pl.pallas_call(
        _conv_gate_kernel,
        grid=(b, nd),
        in_specs=[pl.BlockSpec((1, s, LANES), lambda bi, j: (bi, 0, j)),
                  pl.BlockSpec((1, s, LANES), lambda bi, j: (bi, 0, nd + j)),
                  pl.BlockSpec((1, s, LANES), lambda bi, j: (bi, 0, 2 * nd + j)),
                  pl.BlockSpec((3, LANES), lambda bi, j: (0, j))],
        out_specs=pl.BlockSpec((1, s, LANES), lambda bi, j: (bi, 0, j)),
        out_shape=jax.ShapeDtypeStruct((b, s, d), BF16),
        compiler_params=_params("parallel", "parallel"),
        name="conv_gate",
    )(bcx, bcx, bcx, conv_w.astype(F32))


def _sgu_gate_kernel(z_ref, vg_ref, ws_ref, bs_ref, o_ref):
    d = o_ref.shape[1]
    v = z_ref[:, d:]
    ms = jnp.mean(v * v, axis=-1, keepdims=True)
    vn = (v * lax.rsqrt(ms + EPS) * vg_ref[...]).astype(BF16)
    row = lax.broadcasted_iota(jnp.int32, (SGU_CHUNK, SGU_CHUNK), 0)
    col = lax.broadcasted_iota(jnp.int32, (SGU_CHUNK, SGU_CHUNK), 1)
    for g in range(SGU_GROUPS):
        sl = slice(g * LANES, (g + 1) * LANES)
        w = jnp.where(row >= col, ws_ref[g], 0.0).astype(BF16)
        sv = jnp.dot(w, vn[:, sl], preferred_element_type=F32) + bs_ref[:, g:g + 1]
        o_ref[:, sl] = (z_ref[:, sl] * sv).astype(o_ref.dtype)


def sgu_gate(z, v_gain, w_s, b_s):
    t, d2 = z.shape
    d = d2 // 2
    return pl.pallas_call(
        _sgu_gate_kernel,
        grid=(t // SGU_CHUNK,),
        in_specs=[pl.BlockSpec((SGU_CHUNK, d2), lambda i: (i, 0)),
                  pl.BlockSpec((1, d), lambda i: (0, 0)),
                  pl.BlockSpec((SGU_GROUPS, SGU_CHUNK, SGU_CHUNK), lambda i: (0, 0, 0)),
                  pl.BlockSpec((SGU_CHUNK, SGU_GROUPS), lambda i: (0, 0))],
        out_specs=pl.BlockSpec((SGU_CHUNK, d), lambda i: (i, 0)),
        out_shape=jax.ShapeDtypeStruct((t, d), BF16),
        compiler_params=_params("parallel"),
        name="sgu_gate",
    )(z, v_gain.astype(F32).reshape(1, d), w_s.astype(F32), jnp.transpose(b_s.astype(F32)))


def _norm_transpose_kernel(x_ref, g_ref, o_ref):
    x = x_ref[...]
    ms = jnp.mean(x * x, axis=-1, keepdims=True)
    o_ref[...] = jnp.transpose(x * lax.rsqrt(ms + EPS) * g_ref[...]).astype(o_ref.dtype)


def norm_transpose(x, g, *, tm=512):
    m, k = x.shape
    tm = min(tm, m)
    return pl.pallas_call(
        _norm_transpose_kernel,
        grid=(m // tm,),
        in_specs=[pl.BlockSpec((tm, k), lambda i: (i, 0)),
                  pl.BlockSpec((1, k), lambda i: (0, 0))],
        out_specs=pl.BlockSpec((k, tm), lambda i: (0, i)),
        out_shape=jax.ShapeDtypeStruct((k, m), BF16),
        compiler_params=_params("parallel"),
        name="norm_transpose",
    )(x, g.astype(F32).reshape(1, k))


_A_LAST_K1 = tuple(PEER_TOPK // (k2 + 1) - 1 for k2 in range(8))


def _top16(s, t_ref):
    rank = jnp.full(s.shape, RANK_NONE, F32)
    for kk in range(PEER_TOPK):
        mx = jnp.max(s, axis=0, keepdims=True)
        hit = s == mx
        rank = jnp.where(hit, float(kk), rank)
        s = jnp.where(hit, KNOCKED, s)
        t_ref[kk:kk + 1, :] = mx
    return rank


def _pack_bf16_pairs(x):
    return pltpu.bitcast(x.astype(BF16), jnp.uint32)


def _peer_select_kernel(q_ref, keys_ref, r2_ref, n1_ref, a_ref, b_ref, t1_ref, t2_ref):
    row8 = lax.broadcasted_iota(jnp.int32, (8, LANES), 0)

    def head(h, carry):
        base = pl.multiple_of(h * (2 * PEER_HALF), 2 * PEER_HALF)
        q1 = q_ref[pl.ds(base, PEER_HALF), :].astype(BF16)
        q2 = q_ref[pl.ds(base + PEER_HALF, PEER_HALF), :].astype(BF16)
        s1 = jnp.dot(keys_ref[h, 0], q1, preferred_element_type=F32)
        s2 = jnp.dot(keys_ref[h, 1], q2, preferred_element_type=F32)
        rank1 = _top16(s1, t1_ref)
        rank2 = _top16(s2, t2_ref)
        t1 = t1_ref[...]
        t2 = t2_ref[...]
        t1_lo, t1_hi = t1[0:8], t1[8:16]

        cand = [jnp.where(row8 <= _A_LAST_K1[k2], t1_lo + t2[k2:k2 + 1], KNOCKED) for k2 in range(8)]
        cand.append(t1_hi + t2[0:1])
        cand.append(t1[0:1] + t2[8:16])
        work = list(cand)
        thr = None
        for _ in range(PEER_TOPK):
            mx = functools.reduce(jnp.maximum, work)
            thr = jnp.max(mx, axis=0, keepdims=True)
            work = [jnp.where(w == thr, KNOCKED, w) for w in work]
        top = t1[0:1] + t2[0:1]

        sel = [c >= thr for c in cand]
        ex = [jnp.where(sl, jnp.exp(c - top), 0.0) for sl, c in zip(sel, cand)]
        z = jnp.sum(functools.reduce(jnp.add, ex), axis=0, keepdims=True)
        cnt = [jnp.where(sl, 1.0, 0.0) for sl in sel]
        n_lo = functools.reduce(jnp.add, cnt[:8])
        n_lo = n_lo + jnp.where(row8 == 0, jnp.sum(cnt[9], axis=0, keepdims=True), 0.0)
        n_hi = cnt[8]

        n1 = jnp.zeros(rank1.shape, F32)
        for kk in range(PEER_TOPK):
            src = n_lo if kk < 8 else n_hi
            n1 = jnp.where(rank1 == float(kk), src[kk % 8:kk % 8 + 1], n1)

        r2_ref[h] = _pack_bf16_pairs(rank2)
        n1_ref[h] = n1
        a_ref[h] = jnp.exp(s1 - t1[0:1])
        b_ref[h] = _pack_bf16_pairs(jnp.exp(s2 - t2[0:1]) / z)
        return carry

    lax.fori_loop(0, PEER_HEADS, head, 0)


def peer_select(q_t, keys):
    hq, t = q_t.shape
    shape = (PEER_HEADS, PEER_NKEYS, t)
    packed = (PEER_HEADS, PEER_NKEYS // 2, t)
    ospec = pl.BlockSpec((PEER_HEADS, PEER_NKEYS, LANES), lambda i: (0, 0, i))
    pspec = pl.BlockSpec((PEER_HEADS, PEER_NKEYS // 2, LANES), lambda i: (0, 0, i))
    return pl.pallas_call(
        _peer_select_kernel,
        grid=(t // LANES,),
        in_specs=[pl.BlockSpec((hq, LANES), lambda i: (0, i)),
                  pl.BlockSpec((PEER_HEADS, 2, PEER_NKEYS, PEER_HALF), lambda i: (0, 0, 0, 0))],
        out_specs=[pspec, ospec, ospec, pspec],
        out_shape=[jax.ShapeDtypeStruct(packed, jnp.uint32), jax.ShapeDtypeStruct(shape, F32),
                   jax.ShapeDtypeStruct(shape, F32), jax.ShapeDtypeStruct(packed, jnp.uint32)],
        scratch_shapes=[pltpu.VMEM((PEER_TOPK, LANES), F32), pltpu.VMEM((PEER_TOPK, LANES), F32)],
        compiler_params=_params("parallel"),
        name="peer_select",
    )(q_t, keys)


def _gated_rows(at_ref, y_ref, r2_ref, b_ref, n1_ref, a_ref, r0):
    pieces = PEER_NKEYS // BF16_ROWS
    zero = jnp.zeros((BF16_ROWS, LANES), BF16)
    rows = (r0, r0 + 1)

    def bcast_row(ref, h, r, ls):
        return jnp.broadcast_to(ref[h, r:r + 1, ls], (BF16_ROWS, LANES)).astype(BF16)

    for lt in range(at_ref.shape[1] // LANES):
        ls = slice(lt * LANES, (lt + 1) * LANES)
        w = [[zero] * pieces for _ in rows]
        for h in range(PEER_HEADS):
            n1 = [bcast_row(n1_ref, h, r, ls) for r in rows]
            a1 = [bcast_row(a_ref, h, r, ls) for r in rows]
            for p in range(pieces):
                ks = slice(p * BF16_ROWS // 2, (p + 1) * BF16_ROWS // 2)
                r2 = pltpu.bitcast(r2_ref[h, ks, ls], BF16)
                bb = pltpu.bitcast(b_ref[h, ks, ls], BF16)
                for i in range(2):
                    w[i][p] = w[i][p] + jnp.where(r2 < n1[i], bb, zero) * a1[i]
        for i, r in enumerate(rows):
            for p in range(pieces):
                rs = slice(r * PEER_NKEYS + p * BF16_ROWS, r * PEER_NKEYS + (p + 1) * BF16_ROWS)
                y_ref[rs, ls] = _gelu(at_ref[rs, ls]).astype(BF16) * w[i][p]


def _peer_dense_kernel(x_ref, u_ref, vt_ref, r2_ref, b_ref, n1_ref, a_ref, res_ref, o_ref,
                       acc_ref, at_ref, y_ref):
    j = pl.program_id(1)
    x = x_ref[...]
    n_pieces = u_ref.shape[0] // MXU_ROWS
    for i in range(n_pieces):
        piece = slice(i * MXU_ROWS, (i + 1) * MXU_ROWS)
        at_ref[piece, :] = jnp.dot(u_ref[piece, :], x, preferred_element_type=F32)
    acc = jnp.where(j == 0, 0.0, acc_ref[...])
    for i in range(n_pieces):
        piece = slice(i * MXU_ROWS, (i + 1) * MXU_ROWS)
        _gated_rows(at_ref, y_ref, r2_ref, b_ref, n1_ref, a_ref, i * (MXU_ROWS // PEER_NKEYS))
        acc = acc + jnp.dot(vt_ref[:, piece], y_ref[piece, :], preferred_element_type=F32)
    acc_ref[...] = acc

    @pl.when(j == pl.num_programs(1) - 1)
    def _():
        o_ref[...] = res_ref[...] + jnp.transpose(acc_ref[...])


def peer_dense(xn_t, u, v_t, r2, b, n1, a, res, *, tt=512, ec=2048):
    d, t = xn_t.shape
    e = u.shape[0]
    tt = min(tt, t)
    rows = ec // PEER_NKEYS
    assert t % tt == 0 and e % ec == 0 and ec % MXU_ROWS == 0
    sel_full = pl.BlockSpec((PEER_HEADS, PEER_NKEYS // 2, tt), lambda i, j: (0, 0, i))
    sel_rows = pl.BlockSpec((PEER_HEADS, rows, tt), lambda i, j: (0, j, i))
    return pl.pallas_call(
        _peer_dense_kernel,
        grid=(t // tt, e // ec),
        in_specs=[pl.BlockSpec((d, tt), lambda i, j: (0, i)),
                  pl.BlockSpec((ec, d), lambda i, j: (j, 0)),
                  pl.BlockSpec((d, ec), lambda i, j: (0, j)),
                  sel_full, sel_full, sel_rows, sel_rows,
                  pl.BlockSpec((tt, d), lambda i, j: (i, 0))],
        out_specs=pl.BlockSpec((tt, d), lambda i, j: (i, 0)),
        out_shape=jax.ShapeDtypeStruct((t, d), F32),
        scratch_shapes=[pltpu.VMEM((d, tt), F32), pltpu.VMEM((ec, tt), F32),
                        pltpu.VMEM((ec, tt), BF16)],
        compiler_params=_params("parallel", "arbitrary"),
        name="peer_dense",
    )(xn_t, u, v_t, r2, b, n1, a, res)


def peer_ffn(h2, g, w_q, keys, u, v):
    xn_t = norm_transpose(h2, g)
    q_t = matmul(jnp.transpose(w_q).astype(BF16), xn_t)
    r2, n1, a, b = peer_select(q_t, keys.astype(BF16))
    return peer_dense(xn_t, u.astype(BF16), jnp.transpose(v).astype(BF16), r2, b, n1, a, h2)


def _lambda_init(layer_idx):
    return 0.8 - 0.6 * math.exp(-0.3 * layer_idx)


def kernel(x, rel_bias, norm_mix, norm_ffn, attn_w_qkv, attn_q_gain, attn_k_gain, attn_lam_q1,
           attn_lam_k1, attn_lam_q2, attn_lam_k2, attn_sub_gain, attn_w_o, conv_w_in, conv_w,
           conv_w_out, sgu_w_in, sgu_v_gain, sgu_w_s, sgu_b_s, sgu_w_out, peer_w_q, peer_keys,
           peer_u, peer_v):
    bsz, seq, d = x.shape
    t = bsz * seq
    h = x.reshape(t, d).astype(F32)
    depth = norm_mix.shape[0]
    bias = _bias_tables(rel_bias, min(ATTN_BLOCK, seq))
    for i in range(depth):
        m, j = i % N_MIXERS, i // N_MIXERS
        g = norm_mix[i].astype(F32)
        if m == 0:
            qkv = norm_matmul(h, g, attn_w_qkv[j].astype(BF16))
            q, k, v = qk_norm(qkv, attn_q_gain[j], attn_k_gain[j])
            lam_init = _lambda_init(i)
            lam = (jnp.exp(jnp.sum(attn_lam_q1[j].astype(F32) * attn_lam_k1[j].astype(F32)))
                   - jnp.exp(jnp.sum(attn_lam_q2[j].astype(F32) * attn_lam_k2[j].astype(F32)))
                   + lam_init)
            o = diff_attention_core(q.reshape(bsz, seq, d), k.reshape(bsz, seq, d),
                                    v.reshape(bsz, seq, d), bias, lam, attn_sub_gain[j],
                                    lam_init)
            h = matmul_residual(o.reshape(t, d), attn_w_o[j].astype(BF16), h)
        elif m == 1:
            bcx = norm_matmul(h, g, conv_w_in[j].astype(BF16))
            gated = conv_gate(bcx.reshape(bsz, seq, 3 * d), conv_w[j])
            h = matmul_residual(gated.reshape(t, d), conv_w_out[j].astype(BF16), h)
        else:
            z = norm_matmul(h, g, sgu_w_in[j].astype(BF16), act="gelu")
            gated = sgu_gate(z, sgu_v_gain[j], sgu_w_s[j], sgu_b_s[j])
            h = matmul_residual(gated, sgu_w_out[j].astype(BF16), h)
        h = peer_ffn(h, norm_ffn[i], peer_w_q[i], peer_keys[i], peer_u[i], peer_v[i])
    return h.reshape(bsz, seq, d).astype(x.dtype)
```

```python
import functools
import math

import numpy as np
import jax
import jax.numpy as jnp
from jax import lax
from jax.experimental import pallas as pl
from jax.experimental.pallas import tpu as pltpu

F32 = jnp.float32
BF16 = jnp.bfloat16

EPS = 1e-6
LANES = 128
BF16_ROWS = 16
MXU_ROWS = 256
V7X_VMEM_BYTES = 64 * 1024 * 1024
VMEM_LIMIT = V7X_VMEM_BYTES - 8 * 1024 * 1024

N_MIXERS = 3
ATTN_HEADS = 8
ATTN_HEAD_DIM = 64
ATTN_BLOCK = 512
REL_BUCKETS = 32
REL_MAX_DIST = 128
SGU_CHUNK = 128
SGU_GROUPS = 8
PEER_HEADS = 8
PEER_NKEYS = 128
PEER_HALF = 128
PEER_TOPK = 16

MASK_NEG = -1e30
KNOCKED = -3e38
RANK_NONE = 256.0


def _params(*sem):
    return pltpu.CompilerParams(dimension_semantics=sem, vmem_limit_bytes=VMEM_LIMIT)


def _gelu(x):
    return 0.5 * x * (1.0 + jnp.tanh(0.7978845608028654 * (x + 0.044715 * (x * x * x))))


def _norm_matmul_kernel(x_ref, g_ref, w_ref, o_ref, xn_ref, *, act):
    @pl.when(pl.program_id(1) == 0)
    def _():
        x = x_ref[...]
        ms = jnp.mean(x * x, axis=-1, keepdims=True)
        xn_ref[...] = (x * lax.rsqrt(ms + EPS) * g_ref[...]).astype(BF16)

    y = jnp.dot(xn_ref[...], w_ref[...], preferred_element_type=F32)
    if act == "gelu":
        y = _gelu(y)
    o_ref[...] = y.astype(o_ref.dtype)


def norm_matmul(x, g, w, *, act=None, out_dtype=F32, tm=1024, tn=1024):
    m, k = x.shape
    n = w.shape[1]
    tm, tn = min(tm, m), min(tn, n)
    return pl.pallas_call(
        functools.partial(_norm_matmul_kernel, act=act),
        grid=(m // tm, n // tn),
        in_specs=[pl.BlockSpec((tm, k), lambda i, j: (i, 0)),
                  pl.BlockSpec((1, k), lambda i, j: (0, 0)),
                  pl.BlockSpec((k, tn), lambda i, j: (0, j))],
        out_specs=pl.BlockSpec((tm, tn), lambda i, j: (i, j)),
        out_shape=jax.ShapeDtypeStruct((m, n), out_dtype),
        scratch_shapes=[pltpu.VMEM((tm, k), BF16)],
        compiler_params=_params("parallel", "arbitrary"),
        name="norm_matmul",
    )(x, g.reshape(1, k), w)


def _matmul_residual_kernel(a_ref, w_ref, r_ref, o_ref):
    o_ref[...] = r_ref[...] + jnp.dot(a_ref[...], w_ref[...], preferred_element_type=F32)


def matmul_residual(a, w, res, *, tm=1024):
    m, k = a.shape
    n = w.shape[1]
    tm = min(tm, m)
    return pl.pallas_call(
        _matmul_residual_kernel,
        grid=(m // tm,),
        in_specs=[pl.BlockSpec((tm, k), lambda i: (i, 0)),
                  pl.BlockSpec((k, n), lambda i: (0, 0)),
                  pl.BlockSpec((tm, n), lambda i: (i, 0))],
        out_specs=pl.BlockSpec((tm, n), lambda i: (i, 0)),
        out_shape=jax.ShapeDtypeStruct((m, n), F32),
        compiler_params=_params("parallel"),
        name="matmul_residual",
    )(a, w, res)


def _matmul_kernel(a_ref, b_ref, o_ref):
    o_ref[...] = jnp.dot(a_ref[...], b_ref[...], preferred_element_type=F32).astype(o_ref.dtype)


def matmul(a, b, *, out_dtype=F32, tm=1024, tn=1024):
    m, k = a.shape
    n = b.shape[1]
    tm, tn = min(tm, m), min(tn, n)
    return pl.pallas_call(
        _matmul_kernel,
        grid=(n // tn, m // tm),
        in_specs=[pl.BlockSpec((tm, k), lambda j, i: (i, 0)),
                  pl.BlockSpec((k, tn), lambda j, i: (0, j))],
        out_specs=pl.BlockSpec((tm, tn), lambda j, i: (i, j)),
        out_shape=jax.ShapeDtypeStruct((m, n), out_dtype),
        compiler_params=_params("parallel", "arbitrary"),
        name="matmul",
    )(a, b)


def _qk_norm_kernel(q_ref, k_ref, v_ref, qg_ref, kg_ref, qo_ref, ko_ref, vo_ref):
    lane = lax.broadcasted_iota(jnp.int32, (1, LANES), 1)
    low = lane < ATTN_HEAD_DIM

    def group_norm(x, gain):
        sq = x * x
        tot = jnp.sum(sq, axis=-1, keepdims=True)
        lo = jnp.sum(jnp.where(low, sq, 0.0), axis=-1, keepdims=True)
        ms = jnp.where(low, lo, tot - lo) * (1.0 / ATTN_HEAD_DIM)
        return x * lax.rsqrt(ms + EPS) * gain

    for c in range(ATTN_HEADS):
        sl = slice(c * LANES, (c + 1) * LANES)
        qo_ref[0, c, 0] = jnp.transpose(group_norm(q_ref[:, sl], qg_ref[...])).astype(BF16)
        ko_ref[:, sl] = group_norm(k_ref[:, sl], kg_ref[...]).astype(BF16)
        vo_ref[0, c, 0] = jnp.transpose(v_ref[:, sl]).astype(BF16)


def qk_norm(qkv, q_gain, k_gain, bsz, tb):
    t, d3 = qkv.shape
    d = d3 // 3
    nblk = t // bsz // tb
    hd = 2 * ATTN_HEAD_DIM
    scale = ATTN_HEAD_DIM ** -0.5
    qg = jnp.tile(q_gain.astype(F32) * scale, 2).reshape(1, LANES)
    kg = jnp.tile(k_gain.astype(F32), 2).reshape(1, LANES)
    t_shape = jax.ShapeDtypeStruct((bsz, ATTN_HEADS, nblk, hd, tb), BF16)
    t_spec = pl.BlockSpec((1, ATTN_HEADS, 1, hd, tb), lambda i: (i // nblk, 0, i % nblk, 0, 0))
    return pl.pallas_call(
        _qk_norm_kernel,
        grid=(t // tb,),
        in_specs=[pl.BlockSpec((tb, d), lambda i: (i, 0)),
                  pl.BlockSpec((tb, d), lambda i: (i, 1)),
                  pl.BlockSpec((tb, d), lambda i: (i, 2)),
                  pl.BlockSpec((1, LANES), lambda i: (0, 0)),
                  pl.BlockSpec((1, LANES), lambda i: (0, 0))],
        out_specs=[t_spec, pl.BlockSpec((tb, d), lambda i: (i, 0)), t_spec],
        out_shape=[t_shape, jax.ShapeDtypeStruct((t, d), BF16), t_shape],
        compiler_params=_params("parallel"),
        name="qk_norm",
    )(qkv, qkv, qkv, qg, kg)


def _attn_kernel(lam_ref, q_ref, k_ref, v_ref, bias_ref, sg_ref, o_ref,
                 acc0, acc1, m0, l0, m1, l1, *, tb, out_scale):
    qi = pl.program_id(2)
    dim = lax.broadcasted_iota(jnp.int32, (2 * ATTN_HEAD_DIM, 1), 0)
    q = q_ref[0, 0, 0]
    zero = jnp.zeros_like(q)
    qs = (jnp.where(dim < ATTN_HEAD_DIM, q, zero), jnp.where(dim >= ATTN_HEAD_DIM, q, zero))
    state = ((acc0, m0, l0), (acc1, m1, l1))
    for acc, m, l in state:
        acc[...] = jnp.zeros_like(acc)
        m[...] = jnp.full_like(m, MASK_NEG)
        l[...] = jnp.zeros_like(l)

    def body(ki, carry):
        k = k_ref[0, pl.ds(pl.multiple_of(ki * tb, tb), tb), :]
        v = v_ref[0, 0, ki]
        bias = bias_ref[0, jnp.minimum(qi - ki, 2)]
        for qm, (acc, m, l) in zip(qs, state):
            s = jnp.dot(k, qm, preferred_element_type=F32) + bias
            m_old = m[...]
            m_new = jnp.maximum(m_old, jnp.max(s, axis=0, keepdims=True))
            alpha = jnp.exp(m_old - m_new)
            p = jnp.exp(s - m_new)
            l[...] = alpha * l[...] + jnp.sum(p, axis=0, keepdims=True)
            acc[...] = alpha * acc[...] + jnp.dot(v, p.astype(BF16), preferred_element_type=F32)
            m[...] = m_new
        return carry

    lax.fori_loop(0, qi + 1, body, 0)
    lam = lam_ref[0]
    o = acc0[...] / l0[...] - lam * (acc1[...] / l1[...])
    ms = jnp.mean(o * o, axis=0, keepdims=True)
    o = o * lax.rsqrt(ms + EPS) * (sg_ref[...] * out_scale)
    o_ref[0] = jnp.transpose(o).astype(o_ref.dtype)


def _bucket_thresholds(max_dist):
    n = np.arange(max_dist, dtype=np.int32)
    max_exact = REL_BUCKETS // 2
    nf = np.maximum(n, 1).astype(np.float32)
    large = max_exact + (np.log(nf / np.float32(max_exact))
                         / np.float32(math.log(REL_MAX_DIST / max_exact))
                         * np.float32(REL_BUCKETS - max_exact)).astype(np.int32)
    bucket = np.where(n < max_exact, n, np.minimum(large, REL_BUCKETS - 1))
    assert bucket[-1] == REL_BUCKETS - 1 and np.all(np.diff(bucket) >= 0)
    return tuple(int(np.argmax(bucket >= k)) for k in range(1, REL_BUCKETS))


def _bias_tables_kernel(rb_ref, o_ref, *, tb, thresholds):
    h = pl.program_id(0)
    key = lax.broadcasted_iota(jnp.int32, (tb, tb), 0)
    query = lax.broadcasted_iota(jnp.int32, (tb, tb), 1)
    for j in range(2):
        dist = query - key + j * tb
        val = jnp.full((tb, tb), rb_ref[0, h], F32)
        for k, thr in enumerate(thresholds, start=1):
            val = jnp.where(dist >= thr, rb_ref[k, h], val)
        o_ref[0, j] = jnp.where(dist >= 0, val, MASK_NEG)
    o_ref[0, 2] = jnp.full((tb, tb), rb_ref[REL_BUCKETS - 1, h], F32)


def _bias_tables(rel_bias, tb):
    thresholds = _bucket_thresholds(2 * tb)
    assert thresholds[-1] <= tb + 1
    return pl.pallas_call(
        functools.partial(_bias_tables_kernel, tb=tb, thresholds=thresholds),
        grid=(ATTN_HEADS,),
        in_specs=[pl.BlockSpec(memory_space=pltpu.SMEM)],
        out_specs=pl.BlockSpec((1, 3, tb, tb), lambda h: (h, 0, 0, 0)),
        out_shape=jax.ShapeDtypeStruct((ATTN_HEADS, 3, tb, tb), F32),
        compiler_params=_params("parallel"),
        name="rel_bias_tables",
    )(rel_bias.astype(F32))


def diff_attention_core(q_t, k, v_t, bias, lam, sub_gain, lam_init):
    b, s, d = k.shape
    tb = bias.shape[-1]
    nblk = s // tb
    hd = 2 * ATTN_HEAD_DIM
    return pl.pallas_call(
        functools.partial(_attn_kernel, tb=tb, out_scale=1.0 - lam_init),
        grid=(b, ATTN_HEADS, nblk),
        in_specs=[pl.BlockSpec(memory_space=pltpu.SMEM),
                  pl.BlockSpec((1, 1, 1, hd, tb), lambda bi, h, qi: (bi, h, qi, 0, 0)),
                  pl.BlockSpec((1, s, hd), lambda bi, h, qi: (bi, 0, h)),
                  pl.BlockSpec((1, 1, nblk, hd, tb), lambda bi, h, qi: (bi, h, 0, 0, 0)),
                  pl.BlockSpec((1, 3, tb, tb), lambda bi, h, qi: (h, 0, 0, 0)),
                  pl.BlockSpec((hd, 1), lambda bi, h, qi: (0, 0))],
        out_specs=pl.BlockSpec((1, tb, hd), lambda bi, h, qi: (bi, qi, h)),
        out_shape=jax.ShapeDtypeStruct((b, s, d), BF16),
        scratch_shapes=[pltpu.VMEM((hd, tb), F32), pltpu.VMEM((hd, tb), F32),
                        pltpu.VMEM((1, tb), F32), pltpu.VMEM((1, tb), F32),
                        pltpu.VMEM((1, tb), F32), pltpu.VMEM((1, tb), F32)],
        compiler_params=_params("parallel", "parallel", "arbitrary"),
        name="diff_attention",
    )(lam.reshape(1).astype(F32), q_t, k, v_t, bias, sub_gain.astype(F32).reshape(hd, 1))


def _conv_gate_kernel(b_ref, c_ref, x_ref, w_ref, o_ref):
    z = c_ref[0] * x_ref[0]
    row = lax.broadcasted_iota(jnp.int32, z.shape, 0)
    z1 = jnp.where(row >= 1, pltpu.roll(z, 1, 0), 0.0)
    z2 = jnp.where(row >= 2, pltpu.roll(z, 2, 0), 0.0)
    zc = w_ref[0:1, :] * z2 + w_ref[1:2, :] * z1 + w_ref[2:3, :] * z
    o_ref[0] = (b_ref[0] * zc).astype(o_ref.dtype)


def conv_gate(bcx, conv_w):
    b, s, d3 = bcx.shape
    d = d3 // 3
    nd = d // LANES
    return pl.pallas_call(
        _conv_gate_kernel,
        grid=(b, nd),
        in_specs=[pl.BlockSpec((1, s, LANES), lambda bi, j: (bi, 0, j)),
                  pl.BlockSpec((1, s, LANES), lambda bi, j: (bi, 0, nd + j)),
                  pl.BlockSpec((1, s, LANES), lambda bi, j: (bi, 0, 2 * nd + j)),
                  pl.BlockSpec((3, LANES), lambda bi, j: (0, j))],
        out_specs=pl.BlockSpec((1, s, LANES), lambda bi, j: (bi, 0, j)),
        out_shape=jax.ShapeDtypeStruct((b, s, d), BF16),
        compiler_params=_params("parallel", "parallel"),
        name="conv_gate",
    )(bcx, bcx, bcx, conv_w.astype(F32))


def _sgu_gate_kernel(z_ref, vg_ref, ws_ref, bs_ref, o_ref):
    d = o_ref.shape[1]
    v = z_ref[:, d:]
    ms = jnp.mean(v * v, axis=-1, keepdims=True)
    vn = (v * lax.rsqrt(ms + EPS) * vg_ref[...]).astype(BF16)
    row = lax.broadcasted_iota(jnp.int32, (SGU_CHUNK, SGU_CHUNK), 0)
    col = lax.broadcasted_iota(jnp.int32, (SGU_CHUNK, SGU_CHUNK), 1)
    for g in range(SGU_GROUPS):
        sl = slice(g * LANES, (g + 1) * LANES)
        w = jnp.where(row >= col, ws_ref[g], 0.0).astype(BF16)
        sv = jnp.dot(w, vn[:, sl], preferred_element_type=F32) + bs_ref[:, g:g + 1]
        o_ref[:, sl] = (z_ref[:, sl] * sv).astype(o_ref.dtype)


def sgu_gate(z, v_gain, w_s, b_s):
    t, d2 = z.shape
    d = d2 // 2
    return pl.pallas_call(
        _sgu_gate_kernel,
        grid=(t // SGU_CHUNK,),
        in_specs=[pl.BlockSpec((SGU_CHUNK, d2), lambda i: (i, 0)),
                  pl.BlockSpec((1, d), lambda i: (0, 0)),
                  pl.BlockSpec((SGU_GROUPS, SGU_CHUNK, SGU_CHUNK), lambda i: (0, 0, 0)),
                  pl.BlockSpec((SGU_CHUNK, SGU_GROUPS), lambda i: (0, 0))],
        out_specs=pl.BlockSpec((SGU_CHUNK, d), lambda i: (i, 0)),
        out_shape=jax.ShapeDtypeStruct((t, d), BF16),
        compiler_params=_params("parallel"),
        name="sgu_gate",
    )(z, v_gain.astype(F32).reshape(1, d), w_s.astype(F32), jnp.transpose(b_s.astype(F32)))


def _norm_transpose_kernel(x_ref, g_ref, o_ref):
    x = x_ref[...]
    ms = jnp.mean(x * x, axis=-1, keepdims=True)
    o_ref[...] = jnp.transpose(x * lax.rsqrt(ms + EPS) * g_ref[...]).astype(o_ref.dtype)


def norm_transpose(x, g, *, tm=512):
    m, k = x.shape
    tm = min(tm, m)
    return pl.pallas_call(
        _norm_transpose_kernel,
        grid=(m // tm,),
        in_specs=[pl.BlockSpec((tm, k), lambda i: (i, 0)),
                  pl.BlockSpec((1, k), lambda i: (0, 0))],
        out_specs=pl.BlockSpec((k, tm), lambda i: (0, i)),
        out_shape=jax.ShapeDtypeStruct((k, m), BF16),
        compiler_params=_params("parallel"),
        name="norm_transpose",
    )(x, g.astype(F32).reshape(1, k))


_A_LAST_K1 = tuple(PEER_TOPK // (k2 + 1) - 1 for k2 in range(8))


def _top16(scores, t_refs):
    scores = list(scores)
    ranks = [jnp.full(s.shape, RANK_NONE, F32) for s in scores]
    for kk in range(PEER_TOPK):
        for c, t_ref in enumerate(t_refs):
            mx = jnp.max(scores[c], axis=0, keepdims=True)
            hit = scores[c] == mx
            ranks[c] = jnp.where(hit, float(kk), ranks[c])
            scores[c] = jnp.where(hit, KNOCKED, scores[c])
            t_ref[kk:kk + 1, :] = mx
    return ranks


def _pack_bf16_pairs(x):
    return pltpu.bitcast(x.astype(BF16), jnp.uint32)


def _candidates(t1, t2, row8):
    cand = [jnp.where(row8 <= _A_LAST_K1[k2], t1[0:8] + t2[k2:k2 + 1], KNOCKED) for k2 in range(8)]
    cand.append(t1[8:16] + t2[0:1])
    cand.append(t1[0:1] + t2[8:16])
    return cand


def _sixteenth_largest(cands):
    work = [list(c) for c in cands]
    thr = [None] * len(cands)
    for _ in range(PEER_TOPK):
        for i, w in enumerate(work):
            thr[i] = jnp.max(functools.reduce(jnp.maximum, w), axis=0, keepdims=True)
            work[i] = [jnp.where(x == thr[i], KNOCKED, x) for x in w]
    return thr


def _peer_select_kernel(q_ref, keys_ref, r2_ref, n1_ref, a_ref, b_ref, *t_refs):
    row8 = lax.broadcasted_iota(jnp.int32, (8, LANES), 0)

    def head_pair(hp, carry):
        heads = (2 * hp, 2 * hp + 1)
        scores, ranks, tops = [], [], []
        for h in heads:
            base = pl.multiple_of(h * (2 * PEER_HALF), 2 * PEER_HALF)
            q1 = q_ref[pl.ds(base, PEER_HALF), :].astype(BF16)
            q2 = q_ref[pl.ds(base + PEER_HALF, PEER_HALF), :].astype(BF16)
            scores.append((jnp.dot(keys_ref[h, 0], q1, preferred_element_type=F32),
                           jnp.dot(keys_ref[h, 1], q2, preferred_element_type=F32)))
        for i in range(2):
            ranks.append(_top16(scores[i], t_refs[2 * i:2 * i + 2]))
            tops.append((t_refs[2 * i][...], t_refs[2 * i + 1][...]))
        cands = [_candidates(t1, t2, row8) for t1, t2 in tops]
        thrs = _sixteenth_largest(cands)

        for h, (s1, s2), (rank1, rank2), (t1, t2), cand, thr in zip(heads, scores, ranks, tops,
                                                                       cands, thrs):
            top = t1[0:1] + t2[0:1]
            sel = [c >= thr for c in cand]
            ex = [jnp.where(sl, jnp.exp(c - top), 0.0) for sl, c in zip(sel, cand)]
            z = jnp.sum(functools.reduce(jnp.add, ex), axis=0, keepdims=True)
            cnt = [jnp.where(sl, 1.0, 0.0) for sl in sel]
            n_lo = functools.reduce(jnp.add, cnt[:8])
            n_lo = n_lo + jnp.where(row8 == 0, jnp.sum(cnt[9], axis=0, keepdims=True), 0.0)
            n_hi = cnt[8]

            n1 = jnp.zeros(rank1.shape, F32)
            for kk in range(PEER_TOPK):
                src = n_lo if kk < 8 else n_hi
                n1 = jnp.where(rank1 == float(kk), src[kk % 8:kk % 8 + 1], n1)

            r2_ref[h] = _pack_bf16_pairs(rank2)
            n1_ref[h] = n1
            a_ref[h] = jnp.exp(s1 - t1[0:1])
            b_ref[h] = _pack_bf16_pairs(jnp.exp(s2 - t2[0:1]) / z)
        return carry

    lax.fori_loop(0, PEER_HEADS // 2, head_pair, 0)


def peer_select(q_t, keys):
    hq, t = q_t.shape
    shape = (PEER_HEADS, PEER_NKEYS, t)
    packed = (PEER_HEADS, PEER_NKEYS // 2, t)
    ospec = pl.BlockSpec((PEER_HEADS, PEER_NKEYS, LANES), lambda i: (0, 0, i))
    pspec = pl.BlockSpec((PEER_HEADS, PEER_NKEYS // 2, LANES), lambda i: (0, 0, i))
    return pl.pallas_call(
        _peer_select_kernel,
        grid=(t // LANES,),
        in_specs=[pl.BlockSpec((hq, LANES), lambda i: (0, i)),
                  pl.BlockSpec((PEER_HEADS, 2, PEER_NKEYS, PEER_HALF), lambda i: (0, 0, 0, 0))],
        out_specs=[pspec, ospec, ospec, pspec],
        out_shape=[jax.ShapeDtypeStruct(packed, jnp.uint32), jax.ShapeDtypeStruct(shape, F32),
                   jax.ShapeDtypeStruct(shape, F32), jax.ShapeDtypeStruct(packed, jnp.uint32)],
        scratch_shapes=[pltpu.VMEM((PEER_TOPK, LANES), F32)] * 4,
        compiler_params=_params("parallel"),
        name="peer_select",
    )(q_t, keys)


def _gated_rows(at_ref, y_ref, r2_ref, b_ref, n1_ref, a_ref, r0):
    pieces = PEER_NKEYS // BF16_ROWS
    zero = jnp.zeros((BF16_ROWS, LANES), BF16)
    rows = (r0, r0 + 1)

    def bcast_row(ref, h, r, ls):
        return jnp.broadcast_to(ref[h, r:r + 1, ls], (BF16_ROWS, LANES)).astype(BF16)

    for lt in range(at_ref.shape[1] // LANES):
        ls = slice(lt * LANES, (lt + 1) * LANES)
        w = [[zero] * pieces for _ in rows]
        for h in range(PEER_HEADS):
            n1 = [bcast_row(n1_ref, h, r, ls) for r in rows]
            a1 = [bcast_row(a_ref, h, r, ls) for r in rows]
            for p in range(pieces):
                ks = slice(p * BF16_ROWS // 2, (p + 1) * BF16_ROWS // 2)
                r2 = pltpu.bitcast(r2_ref[h, ks, ls], BF16)
                bb = pltpu.bitcast(b_ref[h, ks, ls], BF16)
                for i in range(2):
                    w[i][p] = w[i][p] + jnp.where(r2 < n1[i], bb, zero) * a1[i]
        for i, r in enumerate(rows):
            for p in range(pieces):
                rs = slice(r * PEER_NKEYS + p * BF16_ROWS, r * PEER_NKEYS + (p + 1) * BF16_ROWS)
                y_ref[rs, ls] = _gelu(at_ref[rs, ls].astype(BF16)) * w[i][p]


def _peer_dense_kernel(x_ref, u_ref, vt_ref, r2_ref, b_ref, n1_ref, a_ref, res_ref, o_ref,
                       acc_ref, at_ref, y_ref):
    j = pl.program_id(1)
    x = x_ref[...]
    n_pieces = u_ref.shape[0] // MXU_ROWS
    for i in range(n_pieces):
        piece = slice(i * MXU_ROWS, (i + 1) * MXU_ROWS)
        at_ref[piece, :] = jnp.dot(u_ref[piece, :], x, preferred_element_type=F32)
    acc = jnp.where(j == 0, 0.0, acc_ref[...])
    for i in range(n_pieces):
        piece = slice(i * MXU_ROWS, (i + 1) * MXU_ROWS)
        _gated_rows(at_ref, y_ref, r2_ref, b_ref, n1_ref, a_ref, i * (MXU_ROWS // PEER_NKEYS))
        acc = acc + jnp.dot(vt_ref[:, piece], y_ref[piece, :], preferred_element_type=F32)
    acc_ref[...] = acc

    @pl.when(j == pl.num_programs(1) - 1)
    def _():
        o_ref[...] = res_ref[...] + jnp.transpose(acc_ref[...])


def peer_dense(xn_t, u, v_t, r2, b, n1, a, res, *, tt=512, ec=2048):
    d, t = xn_t.shape
    e = u.shape[0]
    tt = min(tt, t)
    rows = ec // PEER_NKEYS
    assert t % tt == 0 and e % ec == 0 and ec % MXU_ROWS == 0
    sel_full = pl.BlockSpec((PEER_HEADS, PEER_NKEYS // 2, tt), lambda i, j: (0, 0, i))
    sel_rows = pl.BlockSpec((PEER_HEADS, rows, tt), lambda i, j: (0, j, i))
    return pl.pallas_call(
        _peer_dense_kernel,
        grid=(t // tt, e // ec),
        in_specs=[pl.BlockSpec((d, tt), lambda i, j: (0, i)),
                  pl.BlockSpec((ec, d), lambda i, j: (j, 0)),
                  pl.BlockSpec((d, ec), lambda i, j: (0, j)),
                  sel_full, sel_full, sel_rows, sel_rows,
                  pl.BlockSpec((tt, d), lambda i, j: (i, 0))],
        out_specs=pl.BlockSpec((tt, d), lambda i, j: (i, 0)),
        out_shape=jax.ShapeDtypeStruct((t, d), F32),
        scratch_shapes=[pltpu.VMEM((d, tt), F32), pltpu.VMEM((ec, tt), F32),
                        pltpu.VMEM((ec, tt), BF16)],
        compiler_params=_params("parallel", "arbitrary"),
        name="peer_dense",
    )(xn_t, u, v_t, r2, b, n1, a, res)


def peer_ffn(h2, g, w_q, keys, u, v):
    xn_t = norm_transpose(h2, g)
    q_t = matmul(jnp.transpose(w_q).astype(BF16), xn_t)
    r2, n1, a, b = peer_select(q_t, keys.astype(BF16))
    return peer_dense(xn_t, u.astype(BF16), jnp.transpose(v).astype(BF16), r2, b, n1, a, h2)


def _lambda_init(layer_idx):
    return 0.8 - 0.6 * math.exp(-0.3 * layer_idx)


def kernel(x, rel_bias, norm_mix, norm_ffn, attn_w_qkv, attn_q_gain, attn_k_gain, attn_lam_q1,
           attn_lam_k1, attn_lam_q2, attn_lam_k2, attn_sub_gain, attn_w_o, conv_w_in, conv_w,
           conv_w_out, sgu_w_in, sgu_v_gain, sgu_w_s, sgu_b_s, sgu_w_out, peer_w_q, peer_keys,
           peer_u, peer_v):
    bsz, seq, d = x.shape
    t = bsz * seq
    h = x.reshape(t, d).astype(F32)
    depth = norm_mix.shape[0]
    bias = _bias_tables(rel_bias, min(ATTN_BLOCK, seq))
    for i in range(depth):
        m, j = i % N_MIXERS, i // N_MIXERS
        g = norm_mix[i].astype(F32)
        if m == 0:
            qkv = norm_matmul(h, g, attn_w_qkv[j].astype(BF16))
            q_t, k, v_t = qk_norm(qkv, attn_q_gain[j], attn_k_gain[j], bsz, bias.shape[-1])
            lam_init = _lambda_init(i)
            lam = (jnp.exp(jnp.sum(attn_lam_q1[j].astype(F32) * attn_lam_k1[j].astype(F32)))
                   - jnp.exp(jnp.sum(attn_lam_q2[j].astype(F32) * attn_lam_k2[j].astype(F32)))
                   + lam_init)
            o = diff_attention_core(q_t, k.reshape(bsz, seq, d), v_t, bias, lam,
                                    attn_sub_gain[j], lam_init)
            h = matmul_residual(o.reshape(t, d), attn_w_o[j].astype(BF16), h)
        elif m == 1:
            bcx = norm_matmul(h, g, conv_w_in[j].astype(BF16))
            gated = conv_gate(bcx.reshape(bsz, seq, 3 * d), conv_w[j])
            h = matmul_residual(gated.reshape(t, d), conv_w_out[j].astype(BF16), h)
        else:
            z = norm_matmul(h, g, sgu_w_in[j].astype(BF16), act="gelu")
            gated = sgu_gate(z, sgu_v_gain[j], sgu_w_s[j], sgu_b_s[j])
            h = matmul_residual(gated, sgu_w_out[j].astype(BF16), h)
        h = peer_ffn(h, norm_ffn[i], peer_w_q[i], peer_keys[i], peer_u[i], peer_v[i])
    return h.reshape(bsz, seq, d).astype(x.dtype)
```

```python
import functools
import math

import numpy as np
import jax
import jax.numpy as jnp
from jax import lax
from jax.experimental import pallas as pl
from jax.experimental.pallas import tpu as pltpu

F32 = jnp.float32
BF16 = jnp.bfloat16

EPS = 1e-6
LANES = 128
BF16_ROWS = 16
MXU_ROWS = 256
V7X_VMEM_BYTES = 64 * 1024 * 1024
VMEM_LIMIT = V7X_VMEM_BYTES - 8 * 1024 * 1024

N_MIXERS = 3
ATTN_HEADS = 8
ATTN_HEAD_DIM = 64
ATTN_BLOCK = 512
REL_BUCKETS = 32
REL_MAX_DIST = 128
SGU_CHUNK = 128
SGU_GROUPS = 8
PEER_HEADS = 8
PEER_NKEYS = 128
PEER_HALF = 128
PEER_TOPK = 16

MASK_NEG = -1e30
KNOCKED = -3e38
RANK_NONE = 256.0


def _params(*sem):
    return pltpu.CompilerParams(dimension_semantics=sem, vmem_limit_bytes=VMEM_LIMIT)


def _gelu(x):
    return 0.5 * x * (1.0 + jnp.tanh(0.7978845608028654 * (x + 0.044715 * (x * x * x))))


def _norm_matmul_kernel(x_ref, g_ref, w_ref, o_ref, xn_ref, *, act):
    @pl.when(pl.program_id(1) == 0)
    def _():
        x = x_ref[...]
        ms = jnp.mean(x * x, axis=-1, keepdims=True)
        xn_ref[...] = (x * lax.rsqrt(ms + EPS) * g_ref[...]).astype(BF16)

    y = jnp.dot(xn_ref[...], w_ref[...], preferred_element_type=F32)
    if act == "gelu":
        y = _gelu(y)
    o_ref[...] = y.astype(o_ref.dtype)


def norm_matmul(x, g, w, *, act=None, out_dtype=F32, tm=1024, tn=1024):
    m, k = x.shape
    n = w.shape[1]
    tm, tn = min(tm, m), min(tn, n)
    return pl.pallas_call(
        functools.partial(_norm_matmul_kernel, act=act),
        grid=(m // tm, n // tn),
        in_specs=[pl.BlockSpec((tm, k), lambda i, j: (i, 0)),
                  pl.BlockSpec((1, k), lambda i, j: (0, 0)),
                  pl.BlockSpec((k, tn), lambda i, j: (0, j))],
        out_specs=pl.BlockSpec((tm, tn), lambda i, j: (i, j)),
        out_shape=jax.ShapeDtypeStruct((m, n), out_dtype),
        scratch_shapes=[pltpu.VMEM((tm, k), BF16)],
        compiler_params=_params("parallel", "arbitrary"),
        name="norm_matmul",
    )(x, g.reshape(1, k), w)


def _matmul_residual_kernel(a_ref, w_ref, r_ref, o_ref):
    o_ref[...] = r_ref[...] + jnp.dot(a_ref[...], w_ref[...], preferred_element_type=F32)


def matmul_residual(a, w, res, *, tm=1024):
    m, k = a.shape
    n = w.shape[1]
    tm = min(tm, m)
    return pl.pallas_call(
        _matmul_residual_kernel,
        grid=(m // tm,),
        in_specs=[pl.BlockSpec((tm, k), lambda i: (i, 0)),
                  pl.BlockSpec((k, n), lambda i: (0, 0)),
                  pl.BlockSpec((tm, n), lambda i: (i, 0))],
        out_specs=pl.BlockSpec((tm, n), lambda i: (i, 0)),
        out_shape=jax.ShapeDtypeStruct((m, n), F32),
        compiler_params=_params("parallel"),
        name="matmul_residual",
    )(a, w, res)


def _pack_rows(x):
    n2, l = x.shape
    return lax.bitcast_convert_type(jnp.swapaxes(x.reshape(n2 // 2, 2, l), 1, 2), jnp.uint32)


def _matmul_kernel(a_ref, b_ref, o_ref):
    b = pltpu.bitcast(b_ref[...], BF16)
    o_ref[...] = jnp.dot(a_ref[...], b, preferred_element_type=F32).astype(o_ref.dtype)


def matmul(a, b_packed, *, out_dtype=F32, tm=1024, tn=1024):
    m, k = a.shape
    n = b_packed.shape[1]
    tm, tn = min(tm, m), min(tn, n)
    return pl.pallas_call(
        _matmul_kernel,
        grid=(n // tn, m // tm),
        in_specs=[pl.BlockSpec((tm, k), lambda j, i: (i, 0)),
                  pl.BlockSpec((k // 2, tn), lambda j, i: (0, j))],
        out_specs=pl.BlockSpec((tm, tn), lambda j, i: (i, j)),
        out_shape=jax.ShapeDtypeStruct((m, n), out_dtype),
        compiler_params=_params("parallel", "arbitrary"),
        name="matmul",
    )(a, b_packed)


def _qk_norm_kernel(q_ref, k_ref, v_ref, qg_ref, kg_ref, qo_ref, ko_ref, vo_ref):
    lane = lax.broadcasted_iota(jnp.int32, (1, LANES), 1)
    low = lane < ATTN_HEAD_DIM

    def group_norm(x, gain):
        sq = x * x
        tot = jnp.sum(sq, axis=-1, keepdims=True)
        lo = jnp.sum(jnp.where(low, sq, 0.0), axis=-1, keepdims=True)
        ms = jnp.where(low, lo, tot - lo) * (1.0 / ATTN_HEAD_DIM)
        return x * lax.rsqrt(ms + EPS) * gain

    for c in range(ATTN_HEADS):
        sl = slice(c * LANES, (c + 1) * LANES)
        qo_ref[0, c, 0] = jnp.transpose(group_norm(q_ref[:, sl], qg_ref[...])).astype(BF16)
        ko_ref[:, sl] = group_norm(k_ref[:, sl], kg_ref[...]).astype(BF16)
        vo_ref[0, c, 0] = jnp.transpose(v_ref[:, sl]).astype(BF16)


def qk_norm(qkv, q_gain, k_gain, bsz, tb):
    t, d3 = qkv.shape
    d = d3 // 3
    nblk = t // bsz // tb
    hd = 2 * ATTN_HEAD_DIM
    scale = ATTN_HEAD_DIM ** -0.5
    qg = jnp.tile(q_gain.astype(F32) * scale, 2).reshape(1, LANES)
    kg = jnp.tile(k_gain.astype(F32), 2).reshape(1, LANES)
    t_shape = jax.ShapeDtypeStruct((bsz, ATTN_HEADS, nblk, hd, tb), BF16)
    t_spec = pl.BlockSpec((1, ATTN_HEADS, 1, hd, tb), lambda i: (i // nblk, 0, i % nblk, 0, 0))
    return pl.pallas_call(
        _qk_norm_kernel,
        grid=(t // tb,),
        in_specs=[pl.BlockSpec((tb, d), lambda i: (i, 0)),
                  pl.BlockSpec((tb, d), lambda i: (i, 1)),
                  pl.BlockSpec((tb, d), lambda i: (i, 2)),
                  pl.BlockSpec((1, LANES), lambda i: (0, 0)),
                  pl.BlockSpec((1, LANES), lambda i: (0, 0))],
        out_specs=[t_spec, pl.BlockSpec((tb, d), lambda i: (i, 0)), t_spec],
        out_shape=[t_shape, jax.ShapeDtypeStruct((t, d), BF16), t_shape],
        compiler_params=_params("parallel"),
        name="qk_norm",
    )(qkv, qkv, qkv, qg, kg)


def _attn_kernel(lam_ref, q_ref, k_ref, v_ref, bias_ref, sg_ref, o_ref,
                 acc0, acc1, m0, l0, m1, l1, *, tb, out_scale):
    qi = pl.program_id(2)
    dim = lax.broadcasted_iota(jnp.int32, (2 * ATTN_HEAD_DIM, 1), 0)
    q = q_ref[0, 0, 0]
    zero = jnp.zeros_like(q)
    qs = (jnp.where(dim < ATTN_HEAD_DIM, q, zero), jnp.where(dim >= ATTN_HEAD_DIM, q, zero))
    state = ((acc0, m0, l0), (acc1, m1, l1))
    for acc, m, l in state:
        acc[...] = jnp.zeros_like(acc)
        m[...] = jnp.full_like(m, MASK_NEG)
        l[...] = jnp.zeros_like(l)

    def body(ki, carry):
        k = k_ref[0, pl.ds(pl.multiple_of(ki * tb, tb), tb), :]
        v = v_ref[0, 0, ki]
        bias = bias_ref[0, jnp.minimum(qi - ki, 2)]
        s = [jnp.dot(k, qm, preferred_element_type=F32) + bias for qm in qs]
        m_old = [m[...] for _, m, _ in state]
        m_new = [jnp.maximum(mo, jnp.max(si, axis=0, keepdims=True)) for mo, si in zip(m_old, s)]
        alpha = [jnp.exp(mo - mn) for mo, mn in zip(m_old, m_new)]
        p = [jnp.exp(si - mn) for si, mn in zip(s, m_new)]
        pv = [jnp.dot(v, pi.astype(BF16), preferred_element_type=F32) for pi in p]
        for i, (acc, m, l) in enumerate(state):
            l[...] = alpha[i] * l[...] + jnp.sum(p[i], axis=0, keepdims=True)
            acc[...] = alpha[i] * acc[...] + pv[i]
            m[...] = m_new[i]
        return carry

    lax.fori_loop(0, qi + 1, body, 0)
    lam = lam_ref[0]
    o = acc0[...] / l0[...] - lam * (acc1[...] / l1[...])
    ms = jnp.mean(o * o, axis=0, keepdims=True)
    o = o * lax.rsqrt(ms + EPS) * (sg_ref[...] * out_scale)
    o_ref[0] = jnp.transpose(o).astype(o_ref.dtype)


def _bucket_thresholds(max_dist):
    n = np.arange(max_dist, dtype=np.int32)
    max_exact = REL_BUCKETS // 2
    nf = np.maximum(n, 1).astype(np.float32)
    large = max_exact + (np.log(nf / np.float32(max_exact))
                         / np.float32(math.log(REL_MAX_DIST / max_exact))
                         * np.float32(REL_BUCKETS - max_exact)).astype(np.int32)
    bucket = np.where(n < max_exact, n, np.minimum(large, REL_BUCKETS - 1))
    assert bucket[-1] == REL_BUCKETS - 1 and np.all(np.diff(bucket) >= 0)
    return tuple(int(np.argmax(bucket >= k)) for k in range(1, REL_BUCKETS))


def _bias_tables_kernel(rb_ref, o_ref, *, tb, thresholds):
    h = pl.program_id(0)
    key = lax.broadcasted_iota(jnp.int32, (tb, tb), 0)
    query = lax.broadcasted_iota(jnp.int32, (tb, tb), 1)
    for j in range(2):
        dist = query - key + j * tb
        val = jnp.full((tb, tb), rb_ref[0, h], F32)
        for k, thr in enumerate(thresholds, start=1):
            val = jnp.where(dist >= thr, rb_ref[k, h], val)
        o_ref[0, j] = jnp.where(dist >= 0, val, MASK_NEG)
    o_ref[0, 2] = jnp.full((tb, tb), rb_ref[REL_BUCKETS - 1, h], F32)


def _bias_tables(rel_bias, tb):
    thresholds = _bucket_thresholds(2 * tb)
    assert thresholds[-1] <= tb + 1
    return pl.pallas_call(
        functools.partial(_bias_tables_kernel, tb=tb, thresholds=thresholds),
        grid=(ATTN_HEADS,),
        in_specs=[pl.BlockSpec(memory_space=pltpu.SMEM)],
        out_specs=pl.BlockSpec((1, 3, tb, tb), lambda h: (h, 0, 0, 0)),
        out_shape=jax.ShapeDtypeStruct((ATTN_HEADS, 3, tb, tb), F32),
        compiler_params=_params("parallel"),
        name="rel_bias_tables",
    )(rel_bias.astype(F32))


def diff_attention_core(q_t, k, v_t, bias, lam, sub_gain, lam_init):
    b, s, d = k.shape
    tb = bias.shape[-1]
    nblk = s // tb
    hd = 2 * ATTN_HEAD_DIM
    return pl.pallas_call(
        functools.partial(_attn_kernel, tb=tb, out_scale=1.0 - lam_init),
        grid=(b, ATTN_HEADS, nblk),
        in_specs=[pl.BlockSpec(memory_space=pltpu.SMEM),
                  pl.BlockSpec((1, 1, 1, hd, tb), lambda bi, h, qi: (bi, h, qi, 0, 0)),
                  pl.BlockSpec((1, s, hd), lambda bi, h, qi: (bi, 0, h)),
                  pl.BlockSpec((1, 1, nblk, hd, tb), lambda bi, h, qi: (bi, h, 0, 0, 0)),
                  pl.BlockSpec((1, 3, tb, tb), lambda bi, h, qi: (h, 0, 0, 0)),
                  pl.BlockSpec((hd, 1), lambda bi, h, qi: (0, 0))],
        out_specs=pl.BlockSpec((1, tb, hd), lambda bi, h, qi: (bi, qi, h)),
        out_shape=jax.ShapeDtypeStruct((b, s, d), BF16),
        scratch_shapes=[pltpu.VMEM((hd, tb), F32), pltpu.VMEM((hd, tb), F32),
                        pltpu.VMEM((1, tb), F32), pltpu.VMEM((1, tb), F32),
                        pltpu.VMEM((1, tb), F32), pltpu.VMEM((1, tb), F32)],
        compiler_params=_params("parallel", "parallel", "arbitrary"),
        name="diff_attention",
    )(lam.reshape(1).astype(F32), q_t, k, v_t, bias, sub_gain.astype(F32).reshape(hd, 1))


def _conv_gate_kernel(b_ref, c_ref, x_ref, w_ref, o_ref):
    z = c_ref[0] * x_ref[0]
    row = lax.broadcasted_iota(jnp.int32, z.shape, 0)
    z1 = jnp.where(row >= 1, pltpu.roll(z, 1, 0), 0.0)
    z2 = jnp.where(row >= 2, pltpu.roll(z, 2, 0), 0.0)
    zc = w_ref[0:1, :] * z2 + w_ref[1:2, :] * z1 + w_ref[2:3, :] * z
    o_ref[0] = (b_ref[0] * zc).astype(o_ref.dtype)


def conv_gate(bcx, conv_w):
    b, s, d3 = bcx.shape
    d = d3 // 3
    nd = d // LANES
    return pl.pallas_call(
        _conv_gate_kernel,
        grid=(b, nd),
        in_specs=[pl.BlockSpec((1, s, LANES), lambda bi, j: (bi, 0, j)),
                  pl.BlockSpec((1, s, LANES), lambda bi, j: (bi, 0, nd + j)),
                  pl.BlockSpec((1, s, LANES), lambda bi, j: (bi, 0, 2 * nd + j)),
                  pl.BlockSpec((3, LANES), lambda bi, j: (0, j))],
        out_specs=pl.BlockSpec((1, s, LANES), lambda bi, j: (bi, 0, j)),
        out_shape=jax.ShapeDtypeStruct((b, s, d), BF16),
        compiler_params=_params("parallel", "parallel"),
        name="conv_gate",
    )(bcx, bcx, bcx, conv_w.astype(F32))


def _sgu_gate_kernel(z_ref, vg_ref, ws_ref, bs_ref, o_ref):
    d = o_ref.shape[1]
    v = z_ref[:, d:]
    ms = jnp.mean(v * v, axis=-1, keepdims=True)
    vn = (v * lax.rsqrt(ms + EPS) * vg_ref[...]).astype(BF16)
    row = lax.broadcasted_iota(jnp.int32, (SGU_CHUNK, SGU_CHUNK), 0)
    col = lax.broadcasted_iota(jnp.int32, (SGU_CHUNK, SGU_CHUNK), 1)
    for g in range(SGU_GROUPS):
        sl = slice(g * LANES, (g + 1) * LANES)
        w = jnp.where(row >= col, ws_ref[g], 0.0).astype(BF16)
        sv = jnp.dot(w, vn[:, sl], preferred_element_type=F32) + bs_ref[:, g:g + 1]
        o_ref[:, sl] = (z_ref[:, sl] * sv).astype(o_ref.dtype)


def sgu_gate(z, v_gain, w_s, b_s):
    t, d2 = z.shape
    d = d2 // 2
    return pl.pallas_call(
        _sgu_gate_kernel,
        grid=(t // SGU_CHUNK,),
        in_specs=[pl.BlockSpec((SGU_CHUNK, d2), lambda i: (i, 0)),
                  pl.BlockSpec((1, d), lambda i: (0, 0)),
                  pl.BlockSpec((SGU_GROUPS, SGU_CHUNK, SGU_CHUNK), lambda i: (0, 0, 0)),
                  pl.BlockSpec((SGU_CHUNK, SGU_GROUPS), lambda i: (0, 0))],
        out_specs=pl.BlockSpec((SGU_CHUNK, d), lambda i: (i, 0)),
        out_shape=jax.ShapeDtypeStruct((t, d), BF16),
        compiler_params=_params("parallel"),
        name="sgu_gate",
    )(z, v_gain.astype(F32).reshape(1, d), w_s.astype(F32), jnp.transpose(b_s.astype(F32)))


def _norm_transpose_kernel(x_ref, g_ref, o_ref):
    x = x_ref[...]
    ms = jnp.mean(x * x, axis=-1, keepdims=True)
    o_ref[...] = _pack_bf16_pairs(jnp.transpose(x * lax.rsqrt(ms + EPS) * g_ref[...]))


def norm_transpose(x, g, *, tm=512):
    m, k = x.shape
    tm = min(tm, m)
    return pl.pallas_call(
        _norm_transpose_kernel,
        grid=(m // tm,),
        in_specs=[pl.BlockSpec((tm, k), lambda i: (i, 0)),
                  pl.BlockSpec((1, k), lambda i: (0, 0))],
        out_specs=pl.BlockSpec((k // 2, tm), lambda i: (0, i)),
        out_shape=jax.ShapeDtypeStruct((k // 2, m), jnp.uint32),
        compiler_params=_params("parallel"),
        name="norm_transpose",
    )(x, g.astype(F32).reshape(1, k))


_A_LAST_K1 = tuple(PEER_TOPK // (k2 + 1) - 1 for k2 in range(8))


def _top16(scores, t_refs):
    scores = list(scores)
    ranks = [jnp.full(s.shape, RANK_NONE, F32) for s in scores]
    for kk in range(PEER_TOPK):
        for c, t_ref in enumerate(t_refs):
            mx = jnp.max(scores[c], axis=0, keepdims=True)
            hit = scores[c] == mx
            ranks[c] = jnp.where(hit, float(kk), ranks[c])
            scores[c] = jnp.where(hit, KNOCKED, scores[c])
            t_ref[kk:kk + 1, :] = mx
    return ranks


def _pack_bf16_pairs(x):
    return pltpu.bitcast(x.astype(BF16), jnp.uint32)


def _candidates(t1, t2, row8):
    cand = [jnp.where(row8 <= _A_LAST_K1[k2], t1[0:8] + t2[k2:k2 + 1], KNOCKED) for k2 in range(8)]
    cand.append(t1[8:16] + t2[0:1])
    cand.append(t1[0:1] + t2[8:16])
    return cand


def _sixteenth_largest(cands):
    work = [list(c) for c in cands]
    thr = [None] * len(cands)
    for _ in range(PEER_TOPK):
        for i, w in enumerate(work):
            thr[i] = jnp.max(functools.reduce(jnp.maximum, w), axis=0, keepdims=True)
            work[i] = [jnp.where(x == thr[i], KNOCKED, x) for x in w]
    return thr


def _peer_select_kernel(q_ref, keys_ref, r2_ref, n1_ref, a_ref, b_ref, *t_refs):
    row8 = lax.broadcasted_iota(jnp.int32, (8, LANES), 0)

    def head_pair(hp, carry):
        heads = (2 * hp, 2 * hp + 1)
        scores, ranks, tops = [], [], []
        for h in heads:
            base = pl.multiple_of(h * (2 * PEER_HALF), 2 * PEER_HALF)
            q1 = q_ref[pl.ds(base, PEER_HALF), :].astype(BF16)
            q2 = q_ref[pl.ds(base + PEER_HALF, PEER_HALF), :].astype(BF16)
            scores.append((jnp.dot(keys_ref[h, 0], q1, preferred_element_type=F32),
                           jnp.dot(keys_ref[h, 1], q2, preferred_element_type=F32)))
        for i in range(2):
            ranks.append(_top16(scores[i], t_refs[2 * i:2 * i + 2]))
            tops.append((t_refs[2 * i][...], t_refs[2 * i + 1][...]))
        cands = [_candidates(t1, t2, row8) for t1, t2 in tops]
        thrs = _sixteenth_largest(cands)

        for h, (s1, s2), (rank1, rank2), (t1, t2), cand, thr in zip(heads, scores, ranks, tops,
                                                                       cands, thrs):
            top = t1[0:1] + t2[0:1]
            sel = [c >= thr for c in cand]
            ex = [jnp.where(sl, jnp.exp(c - top), 0.0) for sl, c in zip(sel, cand)]
            z = jnp.sum(functools.reduce(jnp.add, ex), axis=0, keepdims=True)
            cnt = [jnp.where(sl, 1.0, 0.0) for sl in sel]
            n_lo = functools.reduce(jnp.add, cnt[:8])
            n_lo = n_lo + jnp.where(row8 == 0, jnp.sum(cnt[9], axis=0, keepdims=True), 0.0)
            n_hi = cnt[8]

            n1 = jnp.zeros(rank1.shape, F32)
            for kk in range(PEER_TOPK):
                src = n_lo if kk < 8 else n_hi
                n1 = jnp.where(rank1 == float(kk), src[kk % 8:kk % 8 + 1], n1)

            r2_ref[h] = _pack_bf16_pairs(rank2)
            n1_ref[h] = n1
            a_ref[h] = jnp.exp(s1 - t1[0:1])
            b_ref[h] = _pack_bf16_pairs(jnp.exp(s2 - t2[0:1]) / z)
        return carry

    lax.fori_loop(0, PEER_HEADS // 2, head_pair, 0)


def peer_select(q_t, keys):
    hq, t = q_t.shape
    shape = (PEER_HEADS, PEER_NKEYS, t)
    packed = (PEER_HEADS, PEER_NKEYS // 2, t)
    ospec = pl.BlockSpec((PEER_HEADS, PEER_NKEYS, LANES), lambda i: (0, 0, i))
    pspec = pl.BlockSpec((PEER_HEADS, PEER_NKEYS // 2, LANES), lambda i: (0, 0, i))
    return pl.pallas_call(
        _peer_select_kernel,
        grid=(t // LANES,),
        in_specs=[pl.BlockSpec((hq, LANES), lambda i: (0, i)),
                  pl.BlockSpec((PEER_HEADS, 2, PEER_NKEYS, PEER_HALF), lambda i: (0, 0, 0, 0))],
        out_specs=[pspec, ospec, ospec, pspec],
        out_shape=[jax.ShapeDtypeStruct(packed, jnp.uint32), jax.ShapeDtypeStruct(shape, F32),
                   jax.ShapeDtypeStruct(shape, F32), jax.ShapeDtypeStruct(packed, jnp.uint32)],
        scratch_shapes=[pltpu.VMEM((PEER_TOPK, LANES), F32)] * 4,
        compiler_params=_params("parallel"),
        name="peer_select",
    )(q_t, keys)


def _gated_rows(at_ref, y_ref, r2_ref, b_ref, n1_ref, a_ref, r0):
    pieces = PEER_NKEYS // BF16_ROWS
    zero = jnp.zeros((BF16_ROWS, LANES), BF16)
    rows = tuple(range(r0, r0 + MXU_ROWS // PEER_NKEYS))

    def bcast_row(ref, h, r, ls):
        return jnp.broadcast_to(ref[h, r:r + 1, ls], (BF16_ROWS, LANES)).astype(BF16)

    for lt in range(at_ref.shape[1] // LANES):
        ls = slice(lt * LANES, (lt + 1) * LANES)
        w = [[zero] * pieces for _ in rows]
        for h in range(PEER_HEADS):
            n1 = [bcast_row(n1_ref, h, r, ls) for r in rows]
            a1 = [bcast_row(a_ref, h, r, ls) for r in rows]
            for p in range(pieces):
                ks = slice(p * BF16_ROWS // 2, (p + 1) * BF16_ROWS // 2)
                r2 = pltpu.bitcast(r2_ref[h, ks, ls], BF16)
                bb = pltpu.bitcast(b_ref[h, ks, ls], BF16)
                for i in range(len(rows)):
                    w[i][p] = w[i][p] + jnp.where(r2 < n1[i], bb, zero) * a1[i]
        for i, r in enumerate(rows):
            for p in range(pieces):
                r_lo = r * PEER_NKEYS + p * BF16_ROWS
                y = _gelu(at_ref[r_lo:r_lo + BF16_ROWS, ls].astype(BF16)) * w[i][p]
                y_ref[r_lo // 2:(r_lo + BF16_ROWS) // 2, ls] = pltpu.bitcast(y, jnp.uint32)


def _peer_dense_kernel(x_ref, u_ref, vt_ref, r2_ref, b_ref, n1_ref, a_ref, res_ref, o_ref,
                       acc_ref, at_ref, y_ref):
    j = pl.program_id(1)
    x = pltpu.bitcast(x_ref[...], BF16)
    n_pieces = at_ref.shape[0] // MXU_ROWS
    for i in range(n_pieces):
        piece = slice(i * MXU_ROWS, (i + 1) * MXU_ROWS)
        words = slice(i * MXU_ROWS // 2, (i + 1) * MXU_ROWS // 2)
        u = pltpu.bitcast(u_ref[words, :], BF16)
        at_ref[piece, :] = jnp.dot(u, x, preferred_element_type=F32)
    acc = jnp.where(j == 0, 0.0, acc_ref[...])
    for i in range(n_pieces):
        piece = slice(i * MXU_ROWS, (i + 1) * MXU_ROWS)
        words = slice(i * MXU_ROWS // 2, (i + 1) * MXU_ROWS // 2)
        _gated_rows(at_ref, y_ref, r2_ref, b_ref, n1_ref, a_ref, i * (MXU_ROWS // PEER_NKEYS))
        vt = pltpu.bitcast(vt_ref[:, piece], BF16)
        y = pltpu.bitcast(y_ref[words, :], BF16)
        acc = acc + jnp.dot(vt, y, preferred_element_type=F32)
    acc_ref[...] = acc

    @pl.when(j == pl.num_programs(1) - 1)
    def _():
        o_ref[...] = res_ref[...] + jnp.transpose(acc_ref[...])


def peer_dense(xn_t, u, v_t, r2, b, n1, a, res, *, tt=512, ec=2048):
    d, t = res.shape[1], res.shape[0]
    e = v_t.shape[1]
    tt = min(tt, t)
    rows = ec // PEER_NKEYS
    assert t % tt == 0 and e % ec == 0 and ec % MXU_ROWS == 0
    sel_full = pl.BlockSpec((PEER_HEADS, PEER_NKEYS // 2, tt), lambda i, j: (0, 0, i))
    sel_rows = pl.BlockSpec((PEER_HEADS, rows, tt), lambda i, j: (0, j, i))
    return pl.pallas_call(
        _peer_dense_kernel,
        grid=(t // tt, e // ec),
        in_specs=[pl.BlockSpec((d // 2, tt), lambda i, j: (0, i)),
                  pl.BlockSpec((ec // 2, d), lambda i, j: (j, 0)),
                  pl.BlockSpec((d // 2, ec), lambda i, j: (0, j)),
                  sel_full, sel_full, sel_rows, sel_rows,
                  pl.BlockSpec((tt, d), lambda i, j: (i, 0))],
        out_specs=pl.BlockSpec((tt, d), lambda i, j: (i, 0)),
        out_shape=jax.ShapeDtypeStruct((t, d), F32),
        scratch_shapes=[pltpu.VMEM((d, tt), F32), pltpu.VMEM((ec, tt), F32),
                        pltpu.VMEM((ec // 2, tt), jnp.uint32)],
        compiler_params=_params("parallel", "arbitrary"),
        name="peer_dense",
    )(xn_t, u, v_t, r2, b, n1, a, res)


def peer_ffn(h2, g, w_q, keys, u, v):
    xn_t = norm_transpose(h2, g)
    q_t = matmul(jnp.transpose(w_q).astype(BF16), xn_t)
    r2, n1, a, b = peer_select(q_t, keys.astype(BF16))
    return peer_dense(xn_t, _pack_rows(u.astype(BF16)), _pack_rows(jnp.transpose(v).astype(BF16)),
                      r2, b, n1, a, h2)


def _lambda_init(layer_idx):
    return 0.8 - 0.6 * math.exp(-0.3 * layer_idx)


def kernel(x, rel_bias, norm_mix, norm_ffn, attn_w_qkv, attn_q_gain, attn_k_gain, attn_lam_q1,
           attn_lam_k1, attn_lam_q2, attn_lam_k2, attn_sub_gain, attn_w_o, conv_w_in, conv_w,
           conv_w_out, sgu_w_in, sgu_v_gain, sgu_w_s, sgu_b_s, sgu_w_out, peer_w_q, peer_keys,
           peer_u, peer_v):
    bsz, seq, d = x.shape
    t = bsz * seq
    h = x.reshape(t, d).astype(F32)
    depth = norm_mix.shape[0]
    bias = _bias_tables(rel_bias, min(ATTN_BLOCK, seq))
    for i in range(depth):
        m, j = i % N_MIXERS, i // N_MIXERS
        g = norm_mix[i].astype(F32)
        if m == 0:
            qkv = norm_matmul(h, g, attn_w_qkv[j].astype(BF16))
            q_t, k, v_t = qk_norm(qkv, attn_q_gain[j], attn_k_gain[j], bsz, bias.shape[-1])
            lam_init = _lambda_init(i)
            lam = (jnp.exp(jnp.sum(attn_lam_q1[j].astype(F32) * attn_lam_k1[j].astype(F32)))
                   - jnp.exp(jnp.sum(attn_lam_q2[j].astype(F32) * attn_lam_k2[j].astype(F32)))
                   + lam_init)
            o = diff_attention_core(q_t, k.reshape(bsz, seq, d), v_t, bias, lam,
                                    attn_sub_gain[j], lam_init)
            h = matmul_residual(o.reshape(t, d), attn_w_o[j].astype(BF16), h)
        elif m == 1:
            bcx = norm_matmul(h, g, conv_w_in[j].astype(BF16))
            gated = conv_gate(bcx.reshape(bsz, seq, 3 * d), conv_w[j])
            h = matmul_residual(gated.reshape(t, d), conv_w_out[j].astype(BF16), h)
        else:
            z = norm_matmul(h, g, sgu_w_in[j].astype(BF16), act="gelu")
            gated = sgu_gate(z, sgu_v_gain[j], sgu_w_s[j], sgu_b_s[j])
            h = matmul_residual(gated, sgu_w_out[j].astype(BF16), h)
        h = peer_ffn(h, norm_ffn[i], peer_w_q[i], peer_keys[i], peer_u[i], peer_v[i])
    return h.reshape(bsz, seq, d).astype(x.dtype)
```

```python
import functools
import math

import numpy as np
import jax
import jax.numpy as jnp
from jax import lax
from jax.experimental import pallas as pl
from jax.experimental.pallas import tpu as pltpu

F32 = jnp.float32
BF16 = jnp.bfloat16

EPS = 1e-6
LANES = 128
BF16_ROWS = 16
MXU_ROWS = 256
V7X_VMEM_BYTES = 64 * 1024 * 1024
VMEM_LIMIT = V7X_VMEM_BYTES - 8 * 1024 * 1024

N_MIXERS = 3
ATTN_HEADS = 8
ATTN_HEAD_DIM = 64
ATTN_BLOCK = 512
REL_BUCKETS = 32
REL_MAX_DIST = 128
SGU_CHUNK = 128
SGU_GROUPS = 8
PEER_HEADS = 8
PEER_NKEYS = 128
PEER_HALF = 128
PEER_TOPK = 16

MASK_NEG = -1e30
KNOCKED = -3e38
RANK_NONE = 256.0


def _params(*sem):
    return pltpu.CompilerParams(dimension_semantics=sem, vmem_limit_bytes=VMEM_LIMIT)


def _gelu(x):
    return 0.5 * x * (1.0 + jnp.tanh(0.7978845608028654 * (x + 0.044715 * (x * x * x))))


def _norm_matmul_kernel(x_ref, g_ref, w_ref, o_ref, xn_ref, *, act):
    @pl.when(pl.program_id(1) == 0)
    def _():
        x = x_ref[...]
        ms = jnp.mean(x * x, axis=-1, keepdims=True)
        xn_ref[...] = (x * lax.rsqrt(ms + EPS) * g_ref[...]).astype(BF16)

    y = jnp.dot(xn_ref[...], w_ref[...], preferred_element_type=F32)
    if act == "gelu":
        y = _gelu(y)
    o_ref[...] = y.astype(o_ref.dtype)


def norm_matmul(x, g, w, *, act=None, out_dtype=F32, tm=1024, tn=1024):
    m, k = x.shape
    n = w.shape[1]
    tm, tn = min(tm, m), min(tn, n)
    return pl.pallas_call(
        functools.partial(_norm_matmul_kernel, act=act),
        grid=(m // tm, n // tn),
        in_specs=[pl.BlockSpec((tm, k), lambda i, j: (i, 0)),
                  pl.BlockSpec((1, k), lambda i, j: (0, 0)),
                  pl.BlockSpec((k, tn), lambda i, j: (0, j))],
        out_specs=pl.BlockSpec((tm, tn), lambda i, j: (i, j)),
        out_shape=jax.ShapeDtypeStruct((m, n), out_dtype),
        scratch_shapes=[pltpu.VMEM((tm, k), BF16)],
        compiler_params=_params("parallel", "arbitrary"),
        name="norm_matmul",
    )(x, g.reshape(1, k), w)


def _matmul_residual_kernel(a_ref, w_ref, r_ref, o_ref):
    o_ref[...] = r_ref[...] + jnp.dot(a_ref[...], w_ref[...], preferred_element_type=F32)


def matmul_residual(a, w, res, *, tm=1024):
    m, k = a.shape
    n = w.shape[1]
    tm = min(tm, m)
    return pl.pallas_call(
        _matmul_residual_kernel,
        grid=(m // tm,),
        in_specs=[pl.BlockSpec((tm, k), lambda i: (i, 0)),
                  pl.BlockSpec((k, n), lambda i: (0, 0)),
                  pl.BlockSpec((tm, n), lambda i: (i, 0))],
        out_specs=pl.BlockSpec((tm, n), lambda i: (i, 0)),
        out_shape=jax.ShapeDtypeStruct((m, n), F32),
        compiler_params=_params("parallel"),
        name="matmul_residual",
    )(a, w, res)


def _cast_pack_kernel(x_ref, o_ref, *, transpose):
    x = x_ref[...]
    o_ref[...] = _pack_bf16_pairs(jnp.transpose(x) if transpose else x)


def cast_pack(x, *, transpose=False, tm=512):
    r, c = x.shape
    tm = min(tm, r)
    if transpose:
        out_shape, out_spec = (c // 2, r), pl.BlockSpec((c // 2, tm), lambda i: (0, i))
    else:
        out_shape, out_spec = (r // 2, c), pl.BlockSpec((tm // 2, c), lambda i: (i, 0))
    return pl.pallas_call(
        functools.partial(_cast_pack_kernel, transpose=transpose),
        grid=(r // tm,),
        in_specs=[pl.BlockSpec((tm, c), lambda i: (i, 0))],
        out_specs=out_spec,
        out_shape=jax.ShapeDtypeStruct(out_shape, jnp.uint32),
        compiler_params=_params("parallel"),
        name="cast_pack",
    )(x.astype(F32))


def _matmul_kernel(a_ref, b_ref, o_ref):
    b = pltpu.bitcast(b_ref[...], BF16)
    o_ref[...] = jnp.dot(a_ref[...], b, preferred_element_type=F32).astype(o_ref.dtype)


def matmul(a, b_packed, *, out_dtype=F32, tm=1024, tn=1024):
    m, k = a.shape
    n = b_packed.shape[1]
    tm, tn = min(tm, m), min(tn, n)
    return pl.pallas_call(
        _matmul_kernel,
        grid=(n // tn, m // tm),
        in_specs=[pl.BlockSpec((tm, k), lambda j, i: (i, 0)),
                  pl.BlockSpec((k // 2, tn), lambda j, i: (0, j))],
        out_specs=pl.BlockSpec((tm, tn), lambda j, i: (i, j)),
        out_shape=jax.ShapeDtypeStruct((m, n), out_dtype),
        compiler_params=_params("parallel", "arbitrary"),
        name="matmul",
    )(a, b_packed)


def _qk_norm_kernel(q_ref, k_ref, v_ref, qg_ref, kg_ref, qo_ref, ko_ref, vo_ref):
    lane = lax.broadcasted_iota(jnp.int32, (1, LANES), 1)
    low = lane < ATTN_HEAD_DIM

    def group_norm(x, gain):
        sq = x * x
        tot = jnp.sum(sq, axis=-1, keepdims=True)
        lo = jnp.sum(jnp.where(low, sq, 0.0), axis=-1, keepdims=True)
        ms = jnp.where(low, lo, tot - lo) * (1.0 / ATTN_HEAD_DIM)
        return x * lax.rsqrt(ms + EPS) * gain

    for c in range(ATTN_HEADS):
        sl = slice(c * LANES, (c + 1) * LANES)
        qo_ref[0, c, 0] = jnp.transpose(group_norm(q_ref[:, sl], qg_ref[...])).astype(BF16)
        ko_ref[:, sl] = group_norm(k_ref[:, sl], kg_ref[...]).astype(BF16)
        vo_ref[0, c, 0] = jnp.transpose(v_ref[:, sl]).astype(BF16)


def qk_norm(qkv, q_gain, k_gain, bsz, tb):
    t, d3 = qkv.shape
    d = d3 // 3
    nblk = t // bsz // tb
    hd = 2 * ATTN_HEAD_DIM
    scale = ATTN_HEAD_DIM ** -0.5
    qg = jnp.tile(q_gain.astype(F32) * scale, 2).reshape(1, LANES)
    kg = jnp.tile(k_gain.astype(F32), 2).reshape(1, LANES)
    t_shape = jax.ShapeDtypeStruct((bsz, ATTN_HEADS, nblk, hd, tb), BF16)
    t_spec = pl.BlockSpec((1, ATTN_HEADS, 1, hd, tb), lambda i: (i // nblk, 0, i % nblk, 0, 0))
    return pl.pallas_call(
        _qk_norm_kernel,
        grid=(t // tb,),
        in_specs=[pl.BlockSpec((tb, d), lambda i: (i, 0)),
                  pl.BlockSpec((tb, d), lambda i: (i, 1)),
                  pl.BlockSpec((tb, d), lambda i: (i, 2)),
                  pl.BlockSpec((1, LANES), lambda i: (0, 0)),
                  pl.BlockSpec((1, LANES), lambda i: (0, 0))],
        out_specs=[t_spec, pl.BlockSpec((tb, d), lambda i: (i, 0)), t_spec],
        out_shape=[t_shape, jax.ShapeDtypeStruct((t, d), BF16), t_shape],
        compiler_params=_params("parallel"),
        name="qk_norm",
    )(qkv, qkv, qkv, qg, kg)


def _attn_kernel(lam_ref, q_ref, k_ref, v_ref, bias_ref, sg_ref, o_ref,
                 acc0, acc1, m0, l0, m1, l1, *, tb, out_scale):
    qi = pl.program_id(2)
    dim = lax.broadcasted_iota(jnp.int32, (2 * ATTN_HEAD_DIM, 1), 0)
    q = q_ref[0, 0, 0]
    zero = jnp.zeros_like(q)
    qs = (jnp.where(dim < ATTN_HEAD_DIM, q, zero), jnp.where(dim >= ATTN_HEAD_DIM, q, zero))
    state = ((acc0, m0, l0), (acc1, m1, l1))
    for acc, m, l in state:
        acc[...] = jnp.zeros_like(acc)
        m[...] = jnp.full_like(m, MASK_NEG)
        l[...] = jnp.zeros_like(l)

    def body(ki, carry):
        k = k_ref[0, pl.ds(pl.multiple_of(ki * tb, tb), tb), :]
        v = v_ref[0, 0, ki]
        bias = bias_ref[0, jnp.minimum(qi - ki, 2)]
        s = [jnp.dot(k, qm, preferred_element_type=F32) + bias for qm in qs]
        m_old = [m[...] for _, m, _ in state]
        m_new = [jnp.maximum(mo, jnp.max(si, axis=0, keepdims=True)) for mo, si in zip(m_old, s)]
        alpha = [jnp.exp(mo - mn) for mo, mn in zip(m_old, m_new)]
        p = [jnp.exp(si - mn) for si, mn in zip(s, m_new)]
        pv = [jnp.dot(v, pi.astype(BF16), preferred_element_type=F32) for pi in p]
        for i, (acc, m, l) in enumerate(state):
            l[...] = alpha[i] * l[...] + jnp.sum(p[i], axis=0, keepdims=True)
            acc[...] = alpha[i] * acc[...] + pv[i]
            m[...] = m_new[i]
        return carry

    lax.fori_loop(0, qi + 1, body, 0)
    lam = lam_ref[0]
    o = acc0[...] / l0[...] - lam * (acc1[...] / l1[...])
    ms = jnp.mean(o * o, axis=0, keepdims=True)
    o = o * lax.rsqrt(ms + EPS) * (sg_ref[...] * out_scale)
    o_ref[0] = jnp.transpose(o).astype(o_ref.dtype)


def _bucket_thresholds(max_dist):
    n = np.arange(max_dist, dtype=np.int32)
    max_exact = REL_BUCKETS // 2
    nf = np.maximum(n, 1).astype(np.float32)
    large = max_exact + (np.log(nf / np.float32(max_exact))
                         / np.float32(math.log(REL_MAX_DIST / max_exact))
                         * np.float32(REL_BUCKETS - max_exact)).astype(np.int32)
    bucket = np.where(n < max_exact, n, np.minimum(large, REL_BUCKETS - 1))
    assert bucket[-1] == REL_BUCKETS - 1 and np.all(np.diff(bucket) >= 0)
    return tuple(int(np.argmax(bucket >= k)) for k in range(1, REL_BUCKETS))


def _bias_tables_kernel(rb_ref, o_ref, *, tb, thresholds):
    h = pl.program_id(0)
    key = lax.broadcasted_iota(jnp.int32, (tb, tb), 0)
    query = lax.broadcasted_iota(jnp.int32, (tb, tb), 1)
    for j in range(2):
        dist = query - key + j * tb
        val = jnp.full((tb, tb), rb_ref[0, h], F32)
        for k, thr in enumerate(thresholds, start=1):
            val = jnp.where(dist >= thr, rb_ref[k, h], val)
        o_ref[0, j] = jnp.where(dist >= 0, val, MASK_NEG)
    o_ref[0, 2] = jnp.full((tb, tb), rb_ref[REL_BUCKETS - 1, h], F32)


def _bias_tables(rel_bias, tb):
    thresholds = _bucket_thresholds(2 * tb)
    assert thresholds[-1] <= tb + 1
    return pl.pallas_call(
        functools.partial(_bias_tables_kernel, tb=tb, thresholds=thresholds),
        grid=(ATTN_HEADS,),
        in_specs=[pl.BlockSpec(memory_space=pltpu.SMEM)],
        out_specs=pl.BlockSpec((1, 3, tb, tb), lambda h: (h, 0, 0, 0)),
        out_shape=jax.ShapeDtypeStruct((ATTN_HEADS, 3, tb, tb), F32),
        compiler_params=_params("parallel"),
        name="rel_bias_tables",
    )(rel_bias.astype(F32))


def diff_attention_core(q_t, k, v_t, bias, lam, sub_gain, lam_init):
    b, s, d = k.shape
    tb = bias.shape[-1]
    nblk = s // tb
    hd = 2 * ATTN_HEAD_DIM
    return pl.pallas_call(
        functools.partial(_attn_kernel, tb=tb, out_scale=1.0 - lam_init),
        grid=(b, ATTN_HEADS, nblk),
        in_specs=[pl.BlockSpec(memory_space=pltpu.SMEM),
                  pl.BlockSpec((1, 1, 1, hd, tb), lambda bi, h, qi: (bi, h, qi, 0, 0)),
                  pl.BlockSpec((1, s, hd), lambda bi, h, qi: (bi, 0, h)),
                  pl.BlockSpec((1, 1, nblk, hd, tb), lambda bi, h, qi: (bi, h, 0, 0, 0)),
                  pl.BlockSpec((1, 3, tb, tb), lambda bi, h, qi: (h, 0, 0, 0)),
                  pl.BlockSpec((hd, 1), lambda bi, h, qi: (0, 0))],
        out_specs=pl.BlockSpec((1, tb, hd), lambda bi, h, qi: (bi, qi, h)),
        out_shape=jax.ShapeDtypeStruct((b, s, d), BF16),
        scratch_shapes=[pltpu.VMEM((hd, tb), F32), pltpu.VMEM((hd, tb), F32),
                        pltpu.VMEM((1, tb), F32), pltpu.VMEM((1, tb), F32),
                        pltpu.VMEM((1, tb), F32), pltpu.VMEM((1, tb), F32)],
        compiler_params=_params("parallel", "parallel", "arbitrary"),
        name="diff_attention",
    )(lam.reshape(1).astype(F32), q_t, k, v_t, bias, sub_gain.astype(F32).reshape(hd, 1))


def _conv_gate_kernel(b_ref, c_ref, x_ref, w_ref, o_ref):
    z = c_ref[0] * x_ref[0]
    row = lax.broadcasted_iota(jnp.int32, z.shape, 0)
    z1 = jnp.where(row >= 1, pltpu.roll(z, 1, 0), 0.0)
    z2 = jnp.where(row >= 2, pltpu.roll(z, 2, 0), 0.0)
    zc = w_ref[0:1, :] * z2 + w_ref[1:2, :] * z1 + w_ref[2:3, :] * z
    o_ref[0] = (b_ref[0] * zc).astype(o_ref.dtype)


def conv_gate(bcx, conv_w):
    b, s, d3 = bcx.shape
    d = d3 // 3
    nd = d // LANES
    return pl.pallas_call(
        _conv_gate_kernel,
        grid=(b, nd),
        in_specs=[pl.BlockSpec((1, s, LANES), lambda bi, j: (bi, 0, j)),
                  pl.BlockSpec((1, s, LANES), lambda bi, j: (bi, 0, nd + j)),
                  pl.BlockSpec((1, s, LANES), lambda bi, j: (bi, 0, 2 * nd + j)),
                  pl.BlockSpec((3, LANES), lambda bi, j: (0, j))],
        out_specs=pl.BlockSpec((1, s, LANES), lambda bi, j: (bi, 0, j)),
        out_shape=jax.ShapeDtypeStruct((b, s, d), BF16),
        compiler_params=_params("parallel", "parallel"),
        name="conv_gate",
    )(bcx, bcx, bcx, conv_w.astype(F32))


def _sgu_gate_kernel(z_ref, vg_ref, ws_ref, bs_ref, o_ref):
    d = o_ref.shape[1]
    v = z_ref[:, d:]
    ms = jnp.mean(v * v, axis=-1, keepdims=True)
    vn = (v * lax.rsqrt(ms + EPS) * vg_ref[...]).astype(BF16)
    row = lax.broadcasted_iota(jnp.int32, (SGU_CHUNK, SGU_CHUNK), 0)
    col = lax.broadcasted_iota(jnp.int32, (SGU_CHUNK, SGU_CHUNK), 1)
    for g in range(SGU_GROUPS):
        sl = slice(g * LANES, (g + 1) * LANES)
        w = jnp.where(row >= col, ws_ref[g], 0.0).astype(BF16)
        sv = jnp.dot(w, vn[:, sl], preferred_element_type=F32) + bs_ref[:, g:g + 1]
        o_ref[:, sl] = (z_ref[:, sl] * sv).astype(o_ref.dtype)


def sgu_gate(z, v_gain, w_s, b_s):
    t, d2 = z.shape
    d = d2 // 2
    return pl.pallas_call(
        _sgu_gate_kernel,
        grid=(t // SGU_CHUNK,),
        in_specs=[pl.BlockSpec((SGU_CHUNK, d2), lambda i: (i, 0)),
                  pl.BlockSpec((1, d), lambda i: (0, 0)),
                  pl.BlockSpec((SGU_GROUPS, SGU_CHUNK, SGU_CHUNK), lambda i: (0, 0, 0)),
                  pl.BlockSpec((SGU_CHUNK, SGU_GROUPS), lambda i: (0, 0))],
        out_specs=pl.BlockSpec((SGU_CHUNK, d), lambda i: (i, 0)),
        out_shape=jax.ShapeDtypeStruct((t, d), BF16),
        compiler_params=_params("parallel"),
        name="sgu_gate",
    )(z, v_gain.astype(F32).reshape(1, d), w_s.astype(F32), jnp.transpose(b_s.astype(F32)))


def _norm_transpose_kernel(x_ref, g_ref, o_ref):
    x = x_ref[...]
    ms = jnp.mean(x * x, axis=-1, keepdims=True)
    o_ref[...] = _pack_bf16_pairs(jnp.transpose(x * lax.rsqrt(ms + EPS) * g_ref[...]))


def norm_transpose(x, g, *, tm=512):
    m, k = x.shape
    tm = min(tm, m)
    return pl.pallas_call(
        _norm_transpose_kernel,
        grid=(m // tm,),
        in_specs=[pl.BlockSpec((tm, k), lambda i: (i, 0)),
                  pl.BlockSpec((1, k), lambda i: (0, 0))],
        out_specs=pl.BlockSpec((k // 2, tm), lambda i: (0, i)),
        out_shape=jax.ShapeDtypeStruct((k // 2, m), jnp.uint32),
        compiler_params=_params("parallel"),
        name="norm_transpose",
    )(x, g.astype(F32).reshape(1, k))


_A_LAST_K1 = tuple(PEER_TOPK // (k2 + 1) - 1 for k2 in range(8))


def _top16(scores, t_refs):
    scores = list(scores)
    ranks = [jnp.full(s.shape, RANK_NONE, F32) for s in scores]
    for kk in range(PEER_TOPK):
        for c, t_ref in enumerate(t_refs):
            mx = jnp.max(scores[c], axis=0, keepdims=True)
            hit = scores[c] == mx
            ranks[c] = jnp.where(hit, float(kk), ranks[c])
            scores[c] = jnp.where(hit, KNOCKED, scores[c])
            t_ref[kk:kk + 1, :] = mx
    return ranks


def _pack_bf16_pairs(x):
    return pltpu.bitcast(x.astype(BF16), jnp.uint32)


def _candidates(t1, t2, row8):
    cand = [jnp.where(row8 <= _A_LAST_K1[k2], t1[0:8] + t2[k2:k2 + 1], KNOCKED) for k2 in range(8)]
    cand.append(t1[8:16] + t2[0:1])
    cand.append(t1[0:1] + t2[8:16])
    return cand


def _sixteenth_largest(cands):
    work = [list(c) for c in cands]
    thr = [None] * len(cands)
    for _ in range(PEER_TOPK):
        for i, w in enumerate(work):
            thr[i] = jnp.max(functools.reduce(jnp.maximum, w), axis=0, keepdims=True)
            work[i] = [jnp.where(x == thr[i], KNOCKED, x) for x in w]
    return thr


def _peer_select_kernel(q_ref, keys_ref, r2_ref, n1_ref, a_ref, b_ref, *t_refs):
    row8 = lax.broadcasted_iota(jnp.int32, (8, LANES), 0)

    def head_pair(hp, carry):
        heads = (2 * hp, 2 * hp + 1)
        scores, ranks, tops = [], [], []
        for h in heads:
            base = pl.multiple_of(h * (2 * PEER_HALF), 2 * PEER_HALF)
            q1 = q_ref[pl.ds(base, PEER_HALF), :].astype(BF16)
            q2 = q_ref[pl.ds(base + PEER_HALF, PEER_HALF), :].astype(BF16)
            scores.append((jnp.dot(keys_ref[h, 0], q1, preferred_element_type=F32),
                           jnp.dot(keys_ref[h, 1], q2, preferred_element_type=F32)))
        for i in range(2):
            ranks.append(_top16(scores[i], t_refs[2 * i:2 * i + 2]))
            tops.append((t_refs[2 * i][...], t_refs[2 * i + 1][...]))
        cands = [_candidates(t1, t2, row8) for t1, t2 in tops]
        thrs = _sixteenth_largest(cands)

        for h, (s1, s2), (rank1, rank2), (t1, t2), cand, thr in zip(heads, scores, ranks, tops,
                                                                       cands, thrs):
            top = t1[0:1] + t2[0:1]
            sel = [c >= thr for c in cand]
            ex = [jnp.where(sl, jnp.exp(c - top), 0.0) for sl, c in zip(sel, cand)]
            z = jnp.sum(functools.reduce(jnp.add, ex), axis=0, keepdims=True)
            cnt = [jnp.where(sl, 1.0, 0.0) for sl in sel]
            n_lo = functools.reduce(jnp.add, cnt[:8])
            n_lo = n_lo + jnp.where(row8 == 0, jnp.sum(cnt[9], axis=0, keepdims=True), 0.0)
            n_hi = cnt[8]

            n1 = jnp.zeros(rank1.shape, F32)
            for kk in range(PEER_TOPK):
                src = n_lo if kk < 8 else n_hi
                n1 = jnp.where(rank1 == float(kk), src[kk % 8:kk % 8 + 1], n1)

            r2_ref[h] = _pack_bf16_pairs(rank2)
            n1_ref[h] = n1
            a_ref[h] = jnp.exp(s1 - t1[0:1])
            b_ref[h] = _pack_bf16_pairs(jnp.exp(s2 - t2[0:1]) / z)
        return carry

    lax.fori_loop(0, PEER_HEADS // 2, head_pair, 0)


def peer_select(q_t, keys):
    hq, t = q_t.shape
    shape = (PEER_HEADS, PEER_NKEYS, t)
    packed = (PEER_HEADS, PEER_NKEYS // 2, t)
    ospec = pl.BlockSpec((PEER_HEADS, PEER_NKEYS, LANES), lambda i: (0, 0, i))
    pspec = pl.BlockSpec((PEER_HEADS, PEER_NKEYS // 2, LANES), lambda i: (0, 0, i))
    return pl.pallas_call(
        _peer_select_kernel,
        grid=(t // LANES,),
        in_specs=[pl.BlockSpec((hq, LANES), lambda i: (0, i)),
                  pl.BlockSpec((PEER_HEADS, 2, PEER_NKEYS, PEER_HALF), lambda i: (0, 0, 0, 0))],
        out_specs=[pspec, ospec, ospec, pspec],
        out_shape=[jax.ShapeDtypeStruct(packed, jnp.uint32), jax.ShapeDtypeStruct(shape, F32),
                   jax.ShapeDtypeStruct(shape, F32), jax.ShapeDtypeStruct(packed, jnp.uint32)],
        scratch_shapes=[pltpu.VMEM((PEER_TOPK, LANES), F32)] * 4,
        compiler_params=_params("parallel"),
        name="peer_select",
    )(q_t, keys)


def _gated_rows(at_ref, y_ref, r2_ref, b_ref, n1_ref, a_ref, r0):
    pieces = PEER_NKEYS // BF16_ROWS
    zero = jnp.zeros((BF16_ROWS, LANES), BF16)
    rows = tuple(range(r0, r0 + MXU_ROWS // PEER_NKEYS))

    def bcast_row(ref, h, r, ls):
        return jnp.broadcast_to(ref[h, r:r + 1, ls], (BF16_ROWS, LANES)).astype(BF16)

    for lt in range(at_ref.shape[1] // LANES):
        ls = slice(lt * LANES, (lt + 1) * LANES)
        w = [[zero] * pieces for _ in rows]
        for h in range(PEER_HEADS):
            n1 = [bcast_row(n1_ref, h, r, ls) for r in rows]
            a1 = [bcast_row(a_ref, h, r, ls) for r in rows]
            for p in range(pieces):
                ks = slice(p * BF16_ROWS // 2, (p + 1) * BF16_ROWS // 2)
                r2 = pltpu.bitcast(r2_ref[h, ks, ls], BF16)
                bb = pltpu.bitcast(b_ref[h, ks, ls], BF16)
                for i in range(len(rows)):
                    w[i][p] = w[i][p] + jnp.where(r2 < n1[i], bb, zero) * a1[i]
        for i, r in enumerate(rows):
            for p in range(pieces):
                r_lo = r * PEER_NKEYS + p * BF16_ROWS
                y = _gelu(at_ref[r_lo:r_lo + BF16_ROWS, ls].astype(BF16)) * w[i][p]
                y_ref[r_lo // 2:(r_lo + BF16_ROWS) // 2, ls] = pltpu.bitcast(y, jnp.uint32)


def _peer_dense_kernel(x_ref, u_ref, vt_ref, r2_ref, b_ref, n1_ref, a_ref, res_ref, o_ref,
                       acc_ref, at_ref, y_ref):
    j = pl.program_id(1)
    x = pltpu.bitcast(x_ref[...], BF16)
    n_pieces = at_ref.shape[0] // MXU_ROWS
    for i in range(n_pieces):
        piece = slice(i * MXU_ROWS, (i + 1) * MXU_ROWS)
        words = slice(i * MXU_ROWS // 2, (i + 1) * MXU_ROWS // 2)
        u = pltpu.bitcast(u_ref[words, :], BF16)
        at_ref[piece, :] = jnp.dot(u, x, preferred_element_type=F32)
    acc = jnp.where(j == 0, 0.0, acc_ref[...])
    for i in range(n_pieces):
        piece = slice(i * MXU_ROWS, (i + 1) * MXU_ROWS)
        words = slice(i * MXU_ROWS // 2, (i + 1) * MXU_ROWS // 2)
        _gated_rows(at_ref, y_ref, r2_ref, b_ref, n1_ref, a_ref, i * (MXU_ROWS // PEER_NKEYS))
        vt = pltpu.bitcast(vt_ref[:, piece], BF16)
        y = pltpu.bitcast(y_ref[words, :], BF16)
        acc = acc + jnp.dot(vt, y, preferred_element_type=F32)
    acc_ref[...] = acc

    @pl.when(j == pl.num_programs(1) - 1)
    def _():
        o_ref[...] = res_ref[...] + jnp.transpose(acc_ref[...])


def peer_dense(xn_t, u, v_t, r2, b, n1, a, res, *, tt=512, ec=2048):
    d, t = res.shape[1], res.shape[0]
    e = v_t.shape[1]
    tt = min(tt, t)
    rows = ec // PEER_NKEYS
    assert t % tt == 0 and e % ec == 0 and ec % MXU_ROWS == 0
    sel_full = pl.BlockSpec((PEER_HEADS, PEER_NKEYS // 2, tt), lambda i, j: (0, 0, i))
    sel_rows = pl.BlockSpec((PEER_HEADS, rows, tt), lambda i, j: (0, j, i))
    return pl.pallas_call(
        _peer_dense_kernel,
        grid=(t // tt, e // ec),
        in_specs=[pl.BlockSpec((d // 2, tt), lambda i, j: (0, i)),
                  pl.BlockSpec((ec // 2, d), lambda i, j: (j, 0)),
                  pl.BlockSpec((d // 2, ec), lambda i, j: (0, j)),
                  sel_full, sel_full, sel_rows, sel_rows,
                  pl.BlockSpec((tt, d), lambda i, j: (i, 0))],
        out_specs=pl.BlockSpec((tt, d), lambda i, j: (i, 0)),
        out_shape=jax.ShapeDtypeStruct((t, d), F32),
        scratch_shapes=[pltpu.VMEM((d, tt), F32), pltpu.VMEM((ec, tt), F32),
                        pltpu.VMEM((ec // 2, tt), jnp.uint32)],
        compiler_params=_params("parallel", "arbitrary"),
        name="peer_dense",
    )(xn_t, u, v_t, r2, b, n1, a, res)


def peer_ffn(h2, g, w_q, keys, u, v):
    xn_t = norm_transpose(h2, g)
    q_t = matmul(jnp.transpose(w_q).astype(BF16), xn_t)
    r2, n1, a, b = peer_select(q_t, keys.astype(BF16))
    return peer_dense(xn_t, cast_pack(u), cast_pack(v, transpose=True), r2, b, n1, a, h2)


def _lambda_init(layer_idx):
    return 0.8 - 0.6 * math.exp(-0.3 * layer_idx)


def kernel(x, rel_bias, norm_mix, norm_ffn, attn_w_qkv, attn_q_gain, attn_k_gain, attn_lam_q1,
           attn_lam_k1, attn_lam_q2, attn_lam_k2, attn_sub_gain, attn_w_o, conv_w_in, conv_w,
           conv_w_out, sgu_w_in, sgu_v_gain, sgu_w_s, sgu_b_s, sgu_w_out, peer_w_q, peer_keys,
           peer_u, peer_v):
    bsz, seq, d = x.shape
    t = bsz * seq
    h = x.reshape(t, d).astype(F32)
    depth = norm_mix.shape[0]
    bias = _bias_tables(rel_bias, min(ATTN_BLOCK, seq))
    for i in range(depth):
        m, j = i % N_MIXERS, i // N_MIXERS
        g = norm_mix[i].astype(F32)
        if m == 0:
            qkv = norm_matmul(h, g, attn_w_qkv[j].astype(BF16))
            q_t, k, v_t = qk_norm(qkv, attn_q_gain[j], attn_k_gain[j], bsz, bias.shape[-1])
            lam_init = _lambda_init(i)
            lam = (jnp.exp(jnp.sum(attn_lam_q1[j].astype(F32) * attn_lam_k1[j].astype(F32)))
                   - jnp.exp(jnp.sum(attn_lam_q2[j].astype(F32) * attn_lam_k2[j].astype(F32)))
                   + lam_init)
            o = diff_attention_core(q_t, k.reshape(bsz, seq, d), v_t, bias, lam,
                                    attn_sub_gain[j], lam_init)
            h = matmul_residual(o.reshape(t, d), attn_w_o[j].astype(BF16), h)
        elif m == 1:
            bcx = norm_matmul(h, g, conv_w_in[j].astype(BF16))
            gated = conv_gate(bcx.reshape(bsz, seq, 3 * d), conv_w[j])
            h = matmul_residual(gated.reshape(t, d), conv_w_out[j].astype(BF16), h)
        else:
            z = norm_matmul(h, g, sgu_w_in[j].astype(BF16), act="gelu")
            gated = sgu_gate(z, sgu_v_gain[j], sgu_w_s[j], sgu_b_s[j])
            h = matmul_residual(gated, sgu_w_out[j].astype(BF16), h)
        h = peer_ffn(h, norm_ffn[i], peer_w_q[i], peer_keys[i], peer_u[i], peer_v[i])
    return h.reshape(bsz, seq, d).astype(x.dtype)
```

```python
import functools
import math

import numpy as np
import jax
import jax.numpy as jnp
from jax import lax
from jax.experimental import pallas as pl
from jax.experimental.pallas import tpu as pltpu

F32 = jnp.float32
BF16 = jnp.bfloat16

EPS = 1e-6
LANES = 128
BF16_ROWS = 16
MXU_ROWS = 256
V7X_VMEM_BYTES = 64 * 1024 * 1024
VMEM_LIMIT = V7X_VMEM_BYTES - 8 * 1024 * 1024

N_MIXERS = 3
ATTN_HEADS = 8
ATTN_HEAD_DIM = 64
ATTN_BLOCK = 512
REL_BUCKETS = 32
REL_MAX_DIST = 128
SGU_CHUNK = 128
SGU_GROUPS = 8
PEER_HEADS = 8
PEER_NKEYS = 128
PEER_HALF = 128
PEER_TOPK = 16

MASK_NEG = -1e30
KNOCKED = -3e38
RANK_NONE = 256.0


def _params(*sem):
    return pltpu.CompilerParams(dimension_semantics=sem, vmem_limit_bytes=VMEM_LIMIT)


def _gelu(x):
    return 0.5 * x * (1.0 + jnp.tanh(0.7978845608028654 * (x + 0.044715 * (x * x * x))))


def _norm_matmul_kernel(x_ref, g_ref, w_ref, o_ref, xn_ref, *, act):
    @pl.when(pl.program_id(1) == 0)
    def _():
        x = x_ref[...]
        ms = jnp.mean(x * x, axis=-1, keepdims=True)
        xn_ref[...] = (x * lax.rsqrt(ms + EPS) * g_ref[...]).astype(BF16)

    y = jnp.dot(xn_ref[...], w_ref[...], preferred_element_type=F32)
    if act == "gelu":
        y = _gelu(y)
    o_ref[...] = y.astype(o_ref.dtype)


def norm_matmul(x, g, w, *, act=None, out_dtype=F32, tm=1024, tn=1024):
    m, k = x.shape
    n = w.shape[1]
    tm, tn = min(tm, m), min(tn, n)
    return pl.pallas_call(
        functools.partial(_norm_matmul_kernel, act=act),
        grid=(m // tm, n // tn),
        in_specs=[pl.BlockSpec((tm, k), lambda i, j: (i, 0)),
                  pl.BlockSpec((1, k), lambda i, j: (0, 0)),
                  pl.BlockSpec((k, tn), lambda i, j: (0, j))],
        out_specs=pl.BlockSpec((tm, tn), lambda i, j: (i, j)),
        out_shape=jax.ShapeDtypeStruct((m, n), out_dtype),
        scratch_shapes=[pltpu.VMEM((tm, k), BF16)],
        compiler_params=_params("parallel", "arbitrary"),
        name="norm_matmul",
    )(x, g.reshape(1, k), w)


def _matmul_residual_kernel(a_ref, w_ref, r_ref, o_ref):
    o_ref[...] = r_ref[...] + jnp.dot(a_ref[...], w_ref[...], preferred_element_type=F32)


def matmul_residual(a, w, res, *, tm=1024):
    m, k = a.shape
    n = w.shape[1]
    tm = min(tm, m)
    return pl.pallas_call(
        _matmul_residual_kernel,
        grid=(m // tm,),
        in_specs=[pl.BlockSpec((tm, k), lambda i: (i, 0)),
                  pl.BlockSpec((k, n), lambda i: (0, 0)),
                  pl.BlockSpec((tm, n), lambda i: (i, 0))],
        out_specs=pl.BlockSpec((tm, n), lambda i: (i, 0)),
        out_shape=jax.ShapeDtypeStruct((m, n), F32),
        compiler_params=_params("parallel"),
        name="matmul_residual",
    )(a, w, res)


def _cast_pack_kernel(x_ref, o_ref, *, transpose):
    x = x_ref[...]
    o_ref[...] = _pack_bf16_pairs(jnp.transpose(x) if transpose else x)


def cast_pack(x, layer, *, transpose=False, tm=512):
    _, r, c = x.shape
    tm = min(tm, r)
    if transpose:
        out_shape, out_spec = (c // 2, r), pl.BlockSpec((c // 2, tm), lambda i: (0, i))
    else:
        out_shape, out_spec = (r // 2, c), pl.BlockSpec((tm // 2, c), lambda i: (i, 0))
    return pl.pallas_call(
        functools.partial(_cast_pack_kernel, transpose=transpose),
        grid=(r // tm,),
        in_specs=[pl.BlockSpec((None, tm, c), lambda i: (layer, i, 0))],
        out_specs=out_spec,
        out_shape=jax.ShapeDtypeStruct(out_shape, jnp.uint32),
        compiler_params=_params("parallel"),
        name="cast_pack",
    )(x.astype(F32))


def _matmul_kernel(a_ref, b_ref, o_ref):
    b = pltpu.bitcast(b_ref[...], BF16)
    o_ref[...] = jnp.dot(a_ref[...], b, preferred_element_type=F32).astype(o_ref.dtype)


def matmul(a, b_packed, *, out_dtype=F32, tm=1024, tn=1024):
    m, k = a.shape
    n = b_packed.shape[1]
    tm, tn = min(tm, m), min(tn, n)
    return pl.pallas_call(
        _matmul_kernel,
        grid=(n // tn, m // tm),
        in_specs=[pl.BlockSpec((tm, k), lambda j, i: (i, 0)),
                  pl.BlockSpec((k // 2, tn), lambda j, i: (0, j))],
        out_specs=pl.BlockSpec((tm, tn), lambda j, i: (i, j)),
        out_shape=jax.ShapeDtypeStruct((m, n), out_dtype),
        compiler_params=_params("parallel", "arbitrary"),
        name="matmul",
    )(a, b_packed)


def _qk_norm_kernel(q_ref, k_ref, v_ref, qg_ref, kg_ref, qo_ref, ko_ref, vo_ref):
    lane = lax.broadcasted_iota(jnp.int32, (1, LANES), 1)
    low = lane < ATTN_HEAD_DIM

    def group_norm(x, gain):
        sq = x * x
        tot = jnp.sum(sq, axis=-1, keepdims=True)
        lo = jnp.sum(jnp.where(low, sq, 0.0), axis=-1, keepdims=True)
        ms = jnp.where(low, lo, tot - lo) * (1.0 / ATTN_HEAD_DIM)
        return x * lax.rsqrt(ms + EPS) * gain

    for c in range(ATTN_HEADS):
        sl = slice(c * LANES, (c + 1) * LANES)
        qo_ref[0, c, 0] = jnp.transpose(group_norm(q_ref[:, sl], qg_ref[...])).astype(BF16)
        ko_ref[:, sl] = group_norm(k_ref[:, sl], kg_ref[...]).astype(BF16)
        vo_ref[0, c, 0] = jnp.transpose(v_ref[:, sl]).astype(BF16)


def qk_norm(qkv, q_gain, k_gain, bsz, tb):
    t, d3 = qkv.shape
    d = d3 // 3
    nblk = t // bsz // tb
    hd = 2 * ATTN_HEAD_DIM
    scale = ATTN_HEAD_DIM ** -0.5
    qg = jnp.tile(q_gain.astype(F32) * scale, 2).reshape(1, LANES)
    kg = jnp.tile(k_gain.astype(F32), 2).reshape(1, LANES)
    t_shape = jax.ShapeDtypeStruct((bsz, ATTN_HEADS, nblk, hd, tb), BF16)
    t_spec = pl.BlockSpec((1, ATTN_HEADS, 1, hd, tb), lambda i: (i // nblk, 0, i % nblk, 0, 0))
    return pl.pallas_call(
        _qk_norm_kernel,
        grid=(t // tb,),
        in_specs=[pl.BlockSpec((tb, d), lambda i: (i, 0)),
                  pl.BlockSpec((tb, d), lambda i: (i, 1)),
                  pl.BlockSpec((tb, d), lambda i: (i, 2)),
                  pl.BlockSpec((1, LANES), lambda i: (0, 0)),
                  pl.BlockSpec((1, LANES), lambda i: (0, 0))],
        out_specs=[t_spec, pl.BlockSpec((tb, d), lambda i: (i, 0)), t_spec],
        out_shape=[t_shape, jax.ShapeDtypeStruct((t, d), BF16), t_shape],
        compiler_params=_params("parallel"),
        name="qk_norm",
    )(qkv, qkv, qkv, qg, kg)


def _attn_kernel(lam_ref, q_ref, k_ref, v_ref, bias_ref, sg_ref, o_ref,
                 acc0, acc1, m0, l0, m1, l1, *, tb, out_scale):
    qi = pl.program_id(2)
    dim = lax.broadcasted_iota(jnp.int32, (2 * ATTN_HEAD_DIM, 1), 0)
    q = q_ref[0, 0, 0]
    zero = jnp.zeros_like(q)
    qs = (jnp.where(dim < ATTN_HEAD_DIM, q, zero), jnp.where(dim >= ATTN_HEAD_DIM, q, zero))
    state = ((acc0, m0, l0), (acc1, m1, l1))
    for acc, m, l in state:
        acc[...] = jnp.zeros_like(acc)
        m[...] = jnp.full_like(m, MASK_NEG)
        l[...] = jnp.zeros_like(l)

    def body(ki, carry):
        k = k_ref[0, pl.ds(pl.multiple_of(ki * tb, tb), tb), :]
        v = v_ref[0, 0, ki]
        bias = bias_ref[0, jnp.minimum(qi - ki, 2)]
        s = [jnp.dot(k, qm, preferred_element_type=F32) + bias for qm in qs]
        m_old = [m[...] for _, m, _ in state]
        m_new = [jnp.maximum(mo, jnp.max(si, axis=0, keepdims=True)) for mo, si in zip(m_old, s)]
        alpha = [jnp.exp(mo - mn) for mo, mn in zip(m_old, m_new)]
        p = [jnp.exp(si - mn) for si, mn in zip(s, m_new)]
        pv = [jnp.dot(v, pi.astype(BF16), preferred_element_type=F32) for pi in p]
        for i, (acc, m, l) in enumerate(state):
            l[...] = alpha[i] * l[...] + jnp.sum(p[i], axis=0, keepdims=True)
            acc[...] = alpha[i] * acc[...] + pv[i]
            m[...] = m_new[i]
        return carry

    lax.fori_loop(0, qi + 1, body, 0)
    lam = lam_ref[0]
    o = acc0[...] / l0[...] - lam * (acc1[...] / l1[...])
    ms = jnp.mean(o * o, axis=0, keepdims=True)
    o = o * lax.rsqrt(ms + EPS) * (sg_ref[...] * out_scale)
    o_ref[0] = jnp.transpose(o).astype(o_ref.dtype)


def _bucket_thresholds(max_dist):
    n = np.arange(max_dist, dtype=np.int32)
    max_exact = REL_BUCKETS // 2
    nf = np.maximum(n, 1).astype(np.float32)
    large = max_exact + (np.log(nf / np.float32(max_exact))
                         / np.float32(math.log(REL_MAX_DIST / max_exact))
                         * np.float32(REL_BUCKETS - max_exact)).astype(np.int32)
    bucket = np.where(n < max_exact, n, np.minimum(large, REL_BUCKETS - 1))
    assert bucket[-1] == REL_BUCKETS - 1 and np.all(np.diff(bucket) >= 0)
    return tuple(int(np.argmax(bucket >= k)) for k in range(1, REL_BUCKETS))


def _bias_tables_kernel(rb_ref, o_ref, *, tb, thresholds):
    h = pl.program_id(0)
    key = lax.broadcasted_iota(jnp.int32, (tb, tb), 0)
    query = lax.broadcasted_iota(jnp.int32, (tb, tb), 1)
    for j in range(2):
        dist = query - key + j * tb
        val = jnp.full((tb, tb), rb_ref[0, h], F32)
        for k, thr in enumerate(thresholds, start=1):
            val = jnp.where(dist >= thr, rb_ref[k, h], val)
        o_ref[0, j] = jnp.where(dist >= 0, val, MASK_NEG)
    o_ref[0, 2] = jnp.full((tb, tb), rb_ref[REL_BUCKETS - 1, h], F32)


def _bias_tables(rel_bias, tb):
    thresholds = _bucket_thresholds(2 * tb)
    assert thresholds[-1] <= tb + 1
    return pl.pallas_call(
        functools.partial(_bias_tables_kernel, tb=tb, thresholds=thresholds),
        grid=(ATTN_HEADS,),
        in_specs=[pl.BlockSpec(memory_space=pltpu.SMEM)],
        out_specs=pl.BlockSpec((1, 3, tb, tb), lambda h: (h, 0, 0, 0)),
        out_shape=jax.ShapeDtypeStruct((ATTN_HEADS, 3, tb, tb), F32),
        compiler_params=_params("parallel"),
        name="rel_bias_tables",
    )(rel_bias.astype(F32))


def diff_attention_core(q_t, k, v_t, bias, lam, sub_gain, lam_init):
    b, s, d = k.shape
    tb = bias.shape[-1]
    nblk = s // tb
    hd = 2 * ATTN_HEAD_DIM
    return pl.pallas_call(
        functools.partial(_attn_kernel, tb=tb, out_scale=1.0 - lam_init),
        grid=(b, ATTN_HEADS, nblk),
        in_specs=[pl.BlockSpec(memory_space=pltpu.SMEM),
                  pl.BlockSpec((1, 1, 1, hd, tb), lambda bi, h, qi: (bi, h, qi, 0, 0)),
                  pl.BlockSpec((1, s, hd), lambda bi, h, qi: (bi, 0, h)),
                  pl.BlockSpec((1, 1, nblk, hd, tb), lambda bi, h, qi: (bi, h, 0, 0, 0)),
                  pl.BlockSpec((1, 3, tb, tb), lambda bi, h, qi: (h, 0, 0, 0)),
                  pl.BlockSpec((hd, 1), lambda bi, h, qi: (0, 0))],
        out_specs=pl.BlockSpec((1, tb, hd), lambda bi, h, qi: (bi, qi, h)),
        out_shape=jax.ShapeDtypeStruct((b, s, d), BF16),
        scratch_shapes=[pltpu.VMEM((hd, tb), F32), pltpu.VMEM((hd, tb), F32),
                        pltpu.VMEM((1, tb), F32), pltpu.VMEM((1, tb), F32),
                        pltpu.VMEM((1, tb), F32), pltpu.VMEM((1, tb), F32)],
        compiler_params=_params("parallel", "parallel", "arbitrary"),
        name="diff_attention",
    )(lam.reshape(1).astype(F32), q_t, k, v_t, bias, sub_gain.astype(F32).reshape(hd, 1))


def _conv_gate_kernel(b_ref, c_ref, x_ref, w_ref, o_ref):
    z = c_ref[0] * x_ref[0]
    row = lax.broadcasted_iota(jnp.int32, z.shape, 0)
    z1 = jnp.where(row >= 1, pltpu.roll(z, 1, 0), 0.0)
    z2 = jnp.where(row >= 2, pltpu.roll(z, 2, 0), 0.0)
    zc = w_ref[0:1, :] * z2 + w_ref[1:2, :] * z1 + w_ref[2:3, :] * z
    o_ref[0] = (b_ref[0] * zc).astype(o_ref.dtype)


def conv_gate(bcx, conv_w):
    b, s, d3 = bcx.shape
    d = d3 // 3
    nd = d // LANES
    return pl.pallas_call(
        _conv_gate_kernel,
        grid=(b, nd),
        in_specs=[pl.BlockSpec((1, s, LANES), lambda bi, j: (bi, 0, j)),
                  pl.BlockSpec((1, s, LANES), lambda bi, j: (bi, 0, nd + j)),
                  pl.BlockSpec((1, s, LANES), lambda bi, j: (bi, 0, 2 * nd + j)),
                  pl.BlockSpec((3, LANES), lambda bi, j: (0, j))],
        out_specs=pl.BlockSpec((1, s, LANES), lambda bi, j: (bi, 0, j)),
        out_shape=jax.ShapeDtypeStruct((b, s, d), BF16),
        compiler_params=_params("parallel", "parallel"),
        name="conv_gate",
    )(bcx, bcx, bcx, conv_w.astype(F32))


def _sgu_gate_kernel(z_ref, vg_ref, ws_ref, bs_ref, o_ref):
    d = o_ref.shape[1]
    v = z_ref[:, d:]
    ms = jnp.mean(v * v, axis=-1, keepdims=True)
    vn = (v * lax.rsqrt(ms + EPS) * vg_ref[...]).astype(BF16)
    row = lax.broadcasted_iota(jnp.int32, (SGU_CHUNK, SGU_CHUNK), 0)
    col = lax.broadcasted_iota(jnp.int32, (SGU_CHUNK, SGU_CHUNK), 1)
    for g in range(SGU_GROUPS):
        sl = slice(g * LANES, (g + 1) * LANES)
        w = jnp.where(row >= col, ws_ref[g], 0.0).astype(BF16)
        sv = jnp.dot(w, vn[:, sl], preferred_element_type=F32) + bs_ref[:, g:g + 1]
        o_ref[:, sl] = (z_ref[:, sl] * sv).astype(o_ref.dtype)


def sgu_gate(z, v_gain, w_s, b_s):
    t, d2 = z.shape
    d = d2 // 2
    return pl.pallas_call(
        _sgu_gate_kernel,
        grid=(t // SGU_CHUNK,),
        in_specs=[pl.BlockSpec((SGU_CHUNK, d2), lambda i: (i, 0)),
                  pl.BlockSpec((1, d), lambda i: (0, 0)),
                  pl.BlockSpec((SGU_GROUPS, SGU_CHUNK, SGU_CHUNK), lambda i: (0, 0, 0)),
                  pl.BlockSpec((SGU_CHUNK, SGU_GROUPS), lambda i: (0, 0))],
        out_specs=pl.BlockSpec((SGU_CHUNK, d), lambda i: (i, 0)),
        out_shape=jax.ShapeDtypeStruct((t, d), BF16),
        compiler_params=_params("parallel"),
        name="sgu_gate",
    )(z, v_gain.astype(F32).reshape(1, d), w_s.astype(F32), jnp.transpose(b_s.astype(F32)))


def _norm_transpose_kernel(x_ref, g_ref, o_ref):
    x = x_ref[...]
    ms = jnp.mean(x * x, axis=-1, keepdims=True)
    o_ref[...] = _pack_bf16_pairs(jnp.transpose(x * lax.rsqrt(ms + EPS) * g_ref[...]))


def norm_transpose(x, g, *, tm=512):
    m, k = x.shape
    tm = min(tm, m)
    return pl.pallas_call(
        _norm_transpose_kernel,
        grid=(m // tm,),
        in_specs=[pl.BlockSpec((tm, k), lambda i: (i, 0)),
                  pl.BlockSpec((1, k), lambda i: (0, 0))],
        out_specs=pl.BlockSpec((k // 2, tm), lambda i: (0, i)),
        out_shape=jax.ShapeDtypeStruct((k // 2, m), jnp.uint32),
        compiler_params=_params("parallel"),
        name="norm_transpose",
    )(x, g.astype(F32).reshape(1, k))


_A_LAST_K1 = tuple(PEER_TOPK // (k2 + 1) - 1 for k2 in range(8))


def _top16(scores, t_refs):
    scores = list(scores)
    ranks = [jnp.full(s.shape, RANK_NONE, F32) for s in scores]
    for kk in range(PEER_TOPK):
        for c, t_ref in enumerate(t_refs):
            mx = jnp.max(scores[c], axis=0, keepdims=True)
            hit = scores[c] == mx
            ranks[c] = jnp.where(hit, float(kk), ranks[c])
            scores[c] = jnp.where(hit, KNOCKED, scores[c])
            t_ref[kk:kk + 1, :] = mx
    return ranks


def _pack_bf16_pairs(x):
    return pltpu.bitcast(x.astype(BF16), jnp.uint32)


def _candidates(t1, t2, row8):
    cand = [jnp.where(row8 <= _A_LAST_K1[k2], t1[0:8] + t2[k2:k2 + 1], KNOCKED) for k2 in range(8)]
    cand.append(t1[8:16] + t2[0:1])
    cand.append(t1[0:1] + t2[8:16])
    return cand


def _sixteenth_largest(cands):
    work = [list(c) for c in cands]
    thr = [None] * len(cands)
    for _ in range(PEER_TOPK):
        for i, w in enumerate(work):
            thr[i] = jnp.max(functools.reduce(jnp.maximum, w), axis=0, keepdims=True)
            work[i] = [jnp.where(x == thr[i], KNOCKED, x) for x in w]
    return thr


def _peer_select_kernel(q_ref, keys_ref, r2_ref, n1_ref, a_ref, b_ref, *t_refs):
    row8 = lax.broadcasted_iota(jnp.int32, (8, LANES), 0)

    def head_pair(hp, carry):
        heads = (2 * hp, 2 * hp + 1)
        scores, ranks, tops = [], [], []
        for h in heads:
            base = pl.multiple_of(h * (2 * PEER_HALF), 2 * PEER_HALF)
            q1 = q_ref[pl.ds(base, PEER_HALF), :].astype(BF16)
            q2 = q_ref[pl.ds(base + PEER_HALF, PEER_HALF), :].astype(BF16)
            scores.append((jnp.dot(keys_ref[h, 0], q1, preferred_element_type=F32),
                           jnp.dot(keys_ref[h, 1], q2, preferred_element_type=F32)))
        for i in range(2):
            ranks.append(_top16(scores[i], t_refs[2 * i:2 * i + 2]))
            tops.append((t_refs[2 * i][...], t_refs[2 * i + 1][...]))
        cands = [_candidates(t1, t2, row8) for t1, t2 in tops]
        thrs = _sixteenth_largest(cands)

        for h, (s1, s2), (rank1, rank2), (t1, t2), cand, thr in zip(heads, scores, ranks, tops,
                                                                       cands, thrs):
            top = t1[0:1] + t2[0:1]
            sel = [c >= thr for c in cand]
            ex = [jnp.where(sl, jnp.exp(c - top), 0.0) for sl, c in zip(sel, cand)]
            z = jnp.sum(functools.reduce(jnp.add, ex), axis=0, keepdims=True)
            cnt = [jnp.where(sl, 1.0, 0.0) for sl in sel]
            n_lo = functools.reduce(jnp.add, cnt[:8])
            n_lo = n_lo + jnp.where(row8 == 0, jnp.sum(cnt[9], axis=0, keepdims=True), 0.0)
            n_hi = cnt[8]

            n1 = jnp.zeros(rank1.shape, F32)
            for kk in range(PEER_TOPK):
                src = n_lo if kk < 8 else n_hi
                n1 = jnp.where(rank1 == float(kk), src[kk % 8:kk % 8 + 1], n1)

            r2_ref[h] = _pack_bf16_pairs(rank2)
            n1_ref[h] = n1
            a_ref[h] = jnp.exp(s1 - t1[0:1])
            b_ref[h] = _pack_bf16_pairs(jnp.exp(s2 - t2[0:1]) / z)
        return carry

    lax.fori_loop(0, PEER_HEADS // 2, head_pair, 0)


def peer_select(q_t, keys):
    hq, t = q_t.shape
    shape = (PEER_HEADS, PEER_NKEYS, t)
    packed = (PEER_HEADS, PEER_NKEYS // 2, t)
    ospec = pl.BlockSpec((PEER_HEADS, PEER_NKEYS, LANES), lambda i: (0, 0, i))
    pspec = pl.BlockSpec((PEER_HEADS, PEER_NKEYS // 2, LANES), lambda i: (0, 0, i))
    return pl.pallas_call(
        _peer_select_kernel,
        grid=(t // LANES,),
        in_specs=[pl.BlockSpec((hq, LANES), lambda i: (0, i)),
                  pl.BlockSpec((PEER_HEADS, 2, PEER_NKEYS, PEER_HALF), lambda i: (0, 0, 0, 0))],
        out_specs=[pspec, ospec, ospec, pspec],
        out_shape=[jax.ShapeDtypeStruct(packed, jnp.uint32), jax.ShapeDtypeStruct(shape, F32),
                   jax.ShapeDtypeStruct(shape, F32), jax.ShapeDtypeStruct(packed, jnp.uint32)],
        scratch_shapes=[pltpu.VMEM((PEER_TOPK, LANES), F32)] * 4,
        compiler_params=_params("parallel"),
        name="peer_select",
    )(q_t, keys)


def _gated_rows(at_ref, y_ref, r2_ref, b_ref, n1_ref, a_ref, r0):
    pieces = PEER_NKEYS // BF16_ROWS
    zero = jnp.zeros((BF16_ROWS, LANES), BF16)
    rows = tuple(range(r0, r0 + MXU_ROWS // PEER_NKEYS))

    def bcast_row(ref, h, r, ls):
        return jnp.broadcast_to(ref[h, r:r + 1, ls], (BF16_ROWS, LANES)).astype(BF16)

    for lt in range(at_ref.shape[1] // LANES):
        ls = slice(lt * LANES, (lt + 1) * LANES)
        w = [[zero] * pieces for _ in rows]
        for h in range(PEER_HEADS):
            n1 = [bcast_row(n1_ref, h, r, ls) for r in rows]
            a1 = [bcast_row(a_ref, h, r, ls) for r in rows]
            for p in range(pieces):
                ks = slice(p * BF16_ROWS // 2, (p + 1) * BF16_ROWS // 2)
                r2 = pltpu.bitcast(r2_ref[h, ks, ls], BF16)
                bb = pltpu.bitcast(b_ref[h, ks, ls], BF16)
                for i in range(len(rows)):
                    w[i][p] = w[i][p] + jnp.where(r2 < n1[i], bb, zero) * a1[i]
        for i, r in enumerate(rows):
            for p in range(pieces):
                r_lo = r * PEER_NKEYS + p * BF16_ROWS
                y = _gelu(at_ref[r_lo:r_lo + BF16_ROWS, ls].astype(BF16)) * w[i][p]
                y_ref[r_lo // 2:(r_lo + BF16_ROWS) // 2, ls] = pltpu.bitcast(y, jnp.uint32)


def _peer_dense_kernel(x_ref, u_ref, vt_ref, r2_ref, b_ref, n1_ref, a_ref, res_ref, o_ref,
                       acc_ref, at_ref, y_ref):
    j = pl.program_id(1)
    x = pltpu.bitcast(x_ref[...], BF16)
    n_pieces = at_ref.shape[0] // MXU_ROWS
    for i in range(n_pieces):
        piece = slice(i * MXU_ROWS, (i + 1) * MXU_ROWS)
        words = slice(i * MXU_ROWS // 2, (i + 1) * MXU_ROWS // 2)
        u = pltpu.bitcast(u_ref[words, :], BF16)
        at_ref[piece, :] = jnp.dot(u, x, preferred_element_type=F32)
    acc = jnp.where(j == 0, 0.0, acc_ref[...])
    for i in range(n_pieces):
        piece = slice(i * MXU_ROWS, (i + 1) * MXU_ROWS)
        words = slice(i * MXU_ROWS // 2, (i + 1) * MXU_ROWS // 2)
        _gated_rows(at_ref, y_ref, r2_ref, b_ref, n1_ref, a_ref, i * (MXU_ROWS // PEER_NKEYS))
        vt = pltpu.bitcast(vt_ref[:, piece], BF16)
        y = pltpu.bitcast(y_ref[words, :], BF16)
        acc = acc + jnp.dot(vt, y, preferred_element_type=F32)
    acc_ref[...] = acc

    @pl.when(j == pl.num_programs(1) - 1)
    def _():
        o_ref[...] = res_ref[...] + jnp.transpose(acc_ref[...])


def peer_dense(xn_t, u, v_t, r2, b, n1, a, res, *, tt=512, ec=2048):
    d, t = res.shape[1], res.shape[0]
    e = v_t.shape[1]
    tt = min(tt, t)
    rows = ec // PEER_NKEYS
    assert t % tt == 0 and e % ec == 0 and ec % MXU_ROWS == 0
    sel_full = pl.BlockSpec((PEER_HEADS, PEER_NKEYS // 2, tt), lambda i, j: (0, 0, i))
    sel_rows = pl.BlockSpec((PEER_HEADS, rows, tt), lambda i, j: (0, j, i))
    return pl.pallas_call(
        _peer_dense_kernel,
        grid=(t // tt, e // ec),
        in_specs=[pl.BlockSpec((d // 2, tt), lambda i, j: (0, i)),
                  pl.BlockSpec((ec // 2, d), lambda i, j: (j, 0)),
                  pl.BlockSpec((d // 2, ec), lambda i, j: (0, j)),
                  sel_full, sel_full, sel_rows, sel_rows,
                  pl.BlockSpec((tt, d), lambda i, j: (i, 0))],
        out_specs=pl.BlockSpec((tt, d), lambda i, j: (i, 0)),
        out_shape=jax.ShapeDtypeStruct((t, d), F32),
        scratch_shapes=[pltpu.VMEM((d, tt), F32), pltpu.VMEM((ec, tt), F32),
                        pltpu.VMEM((ec // 2, tt), jnp.uint32)],
        compiler_params=_params("parallel", "arbitrary"),
        name="peer_dense",
    )(xn_t, u, v_t, r2, b, n1, a, res)


def peer_ffn(h2, g, w_q, keys, u, v, layer):
    xn_t = norm_transpose(h2, g)
    q_t = matmul(jnp.transpose(w_q).astype(BF16), xn_t)
    r2, n1, a, b = peer_select(q_t, keys.astype(BF16))
    return peer_dense(xn_t, cast_pack(u, layer), cast_pack(v, layer, transpose=True),
                      r2, b, n1, a, h2)


def _lambda_init(layer_idx):
    return 0.8 - 0.6 * math.exp(-0.3 * layer_idx)


def kernel(x, rel_bias, norm_mix, norm_ffn, attn_w_qkv, attn_q_gain, attn_k_gain, attn_lam_q1,
           attn_lam_k1, attn_lam_q2, attn_lam_k2, attn_sub_gain, attn_w_o, conv_w_in, conv_w,
           conv_w_out, sgu_w_in, sgu_v_gain, sgu_w_s, sgu_b_s, sgu_w_out, peer_w_q, peer_keys,
           peer_u, peer_v):
    bsz, seq, d = x.shape
    t = bsz * seq
    h = x.reshape(t, d).astype(F32)
    depth = norm_mix.shape[0]
    bias = _bias_tables(rel_bias, min(ATTN_BLOCK, seq))
    for i in range(depth):
        m, j = i % N_MIXERS, i // N_MIXERS
        g = norm_mix[i].astype(F32)
        if m == 0:
            qkv = norm_matmul(h, g, attn_w_qkv[j].astype(BF16))
            q_t, k, v_t = qk_norm(qkv, attn_q_gain[j], attn_k_gain[j], bsz, bias.shape[-1])
            lam_init = _lambda_init(i)
            lam = (jnp.exp(jnp.sum(attn_lam_q1[j].astype(F32) * attn_lam_k1[j].astype(F32)))
                   - jnp.exp(jnp.sum(attn_lam_q2[j].astype(F32) * attn_lam_k2[j].astype(F32)))
                   + lam_init)
            o = diff_attention_core(q_t, k.reshape(bsz, seq, d), v_t, bias, lam,
                                    attn_sub_gain[j], lam_init)
            h = matmul_residual(o.reshape(t, d), attn_w_o[j].astype(BF16), h)
        elif m == 1:
            bcx = norm_matmul(h, g, conv_w_in[j].astype(BF16))
            gated = conv_gate(bcx.reshape(bsz, seq, 3 * d), conv_w[j])
            h = matmul_residual(gated.reshape(t, d), conv_w_out[j].astype(BF16), h)
        else:
            z = norm_matmul(h, g, sgu_w_in[j].astype(BF16), act="gelu")
            gated = sgu_gate(z, sgu_v_gain[j], sgu_w_s[j], sgu_b_s[j])
            h = matmul_residual(gated, sgu_w_out[j].astype(BF16), h)
        h = peer_ffn(h, norm_ffn[i], peer_w_q[i], peer_keys[i], peer_u, peer_v, i)
    return h.reshape(bsz, seq, d).astype(x.dtype)
```

```python
import functools
import math

import numpy as np
import jax
import jax.numpy as jnp
from jax import lax
from jax.experimental import pallas as pl
from jax.experimental.pallas import tpu as pltpu

F32 = jnp.float32
BF16 = jnp.bfloat16

EPS = 1e-6
LANES = 128
BF16_ROWS = 16
MXU_ROWS = 256
V7X_VMEM_BYTES = 64 * 1024 * 1024
VMEM_LIMIT = V7X_VMEM_BYTES - 8 * 1024 * 1024

N_MIXERS = 3
ATTN_HEADS = 8
ATTN_HEAD_DIM = 64
ATTN_BLOCK = 512
REL_BUCKETS = 32
REL_MAX_DIST = 128
SGU_CHUNK = 128
SGU_GROUPS = 8
PEER_HEADS = 8
PEER_NKEYS = 128
PEER_HALF = 128
PEER_TOPK = 16

MASK_NEG = -1e30
KNOCKED = -3e38
RANK_NONE = 256.0


def _params(*sem):
    return pltpu.CompilerParams(dimension_semantics=sem, vmem_limit_bytes=VMEM_LIMIT)


def _gelu(x):
    return 0.5 * x * (1.0 + jnp.tanh(0.7978845608028654 * (x + 0.044715 * (x * x * x))))


def _norm_matmul_kernel(x_ref, g_ref, w_ref, o_ref, xn_ref, *, act):
    @pl.when(pl.program_id(1) == 0)
    def _():
        x = x_ref[...]
        ms = jnp.mean(x * x, axis=-1, keepdims=True)
        xn_ref[...] = (x * lax.rsqrt(ms + EPS) * g_ref[...]).astype(BF16)

    y = jnp.dot(xn_ref[...], w_ref[...], preferred_element_type=F32)
    if act == "gelu":
        y = _gelu(y)
    o_ref[...] = y.astype(o_ref.dtype)


def norm_matmul(x, g, w, *, act=None, out_dtype=F32, tm=1024, tn=1024):
    m, k = x.shape
    n = w.shape[1]
    tm, tn = min(tm, m), min(tn, n)
    return pl.pallas_call(
        functools.partial(_norm_matmul_kernel, act=act),
        grid=(m // tm, n // tn),
        in_specs=[pl.BlockSpec((tm, k), lambda i, j: (i, 0)),
                  pl.BlockSpec((1, k), lambda i, j: (0, 0)),
                  pl.BlockSpec((k, tn), lambda i, j: (0, j))],
        out_specs=pl.BlockSpec((tm, tn), lambda i, j: (i, j)),
        out_shape=jax.ShapeDtypeStruct((m, n), out_dtype),
        scratch_shapes=[pltpu.VMEM((tm, k), BF16)],
        compiler_params=_params("parallel", "arbitrary"),
        name="norm_matmul",
    )(x, g.reshape(1, k), w)


def _matmul_residual_kernel(a_ref, w_ref, r_ref, o_ref):
    o_ref[...] = r_ref[...] + jnp.dot(a_ref[...], w_ref[...], preferred_element_type=F32)


def matmul_residual(a, w, res, *, tm=1024):
    m, k = a.shape
    n = w.shape[1]
    tm = min(tm, m)
    return pl.pallas_call(
        _matmul_residual_kernel,
        grid=(m // tm,),
        in_specs=[pl.BlockSpec((tm, k), lambda i: (i, 0)),
                  pl.BlockSpec((k, n), lambda i: (0, 0)),
                  pl.BlockSpec((tm, n), lambda i: (i, 0))],
        out_specs=pl.BlockSpec((tm, n), lambda i: (i, 0)),
        out_shape=jax.ShapeDtypeStruct((m, n), F32),
        compiler_params=_params("parallel"),
        name="matmul_residual",
    )(a, w, res)


def _cast_pack_kernel(x_ref, o_ref, *, transpose):
    x = x_ref[...]
    o_ref[...] = _pack_bf16_pairs(jnp.transpose(x) if transpose else x)


def cast_pack(x, layer, *, transpose=False, tm=512):
    _, r, c = x.shape
    tm = min(tm, r)
    if transpose:
        out_shape, out_spec = (c // 2, r), pl.BlockSpec((c // 2, tm), lambda i: (0, i))
    else:
        out_shape, out_spec = (r // 2, c), pl.BlockSpec((tm // 2, c), lambda i: (i, 0))
    return pl.pallas_call(
        functools.partial(_cast_pack_kernel, transpose=transpose),
        grid=(r // tm,),
        in_specs=[pl.BlockSpec((None, tm, c), lambda i: (layer, i, 0))],
        out_specs=out_spec,
        out_shape=jax.ShapeDtypeStruct(out_shape, jnp.uint32),
        compiler_params=_params("parallel"),
        name="cast_pack",
    )(x.astype(F32))


def _matmul_kernel(a_ref, b_ref, o_ref):
    b = pltpu.bitcast(b_ref[...], BF16)
    o_ref[...] = jnp.dot(a_ref[...], b, preferred_element_type=F32).astype(o_ref.dtype)


def matmul(a, b_packed, *, out_dtype=F32, tm=1024, tn=1024):
    m, k = a.shape
    n = b_packed.shape[1]
    tm, tn = min(tm, m), min(tn, n)
    return pl.pallas_call(
        _matmul_kernel,
        grid=(n // tn, m // tm),
        in_specs=[pl.BlockSpec((tm, k), lambda j, i: (i, 0)),
                  pl.BlockSpec((k // 2, tn), lambda j, i: (0, j))],
        out_specs=pl.BlockSpec((tm, tn), lambda j, i: (i, j)),
        out_shape=jax.ShapeDtypeStruct((m, n), out_dtype),
        compiler_params=_params("parallel", "arbitrary"),
        name="matmul",
    )(a, b_packed)


def _qk_norm_kernel(q_ref, k_ref, v_ref, qg_ref, kg_ref, qo_ref, ko_ref, vo_ref):
    lane = lax.broadcasted_iota(jnp.int32, (1, LANES), 1)
    low = lane < ATTN_HEAD_DIM

    def group_norm(x, gain):
        sq = x * x
        tot = jnp.sum(sq, axis=-1, keepdims=True)
        lo = jnp.sum(jnp.where(low, sq, 0.0), axis=-1, keepdims=True)
        ms = jnp.where(low, lo, tot - lo) * (1.0 / ATTN_HEAD_DIM)
        return x * lax.rsqrt(ms + EPS) * gain

    for c in range(ATTN_HEADS):
        sl = slice(c * LANES, (c + 1) * LANES)
        qo_ref[0, c, 0] = jnp.transpose(group_norm(q_ref[:, sl], qg_ref[...])).astype(BF16)
        ko_ref[:, sl] = group_norm(k_ref[:, sl], kg_ref[...]).astype(BF16)
        vo_ref[0, c, 0] = jnp.transpose(v_ref[:, sl]).astype(BF16)


def qk_norm(qkv, q_gain, k_gain, bsz, tb):
    t, d3 = qkv.shape
    d = d3 // 3
    nblk = t // bsz // tb
    hd = 2 * ATTN_HEAD_DIM
    scale = ATTN_HEAD_DIM ** -0.5
    qg = jnp.tile(q_gain.astype(F32) * scale, 2).reshape(1, LANES)
    kg = jnp.tile(k_gain.astype(F32), 2).reshape(1, LANES)
    t_shape = jax.ShapeDtypeStruct((bsz, ATTN_HEADS, nblk, hd, tb), BF16)
    t_spec = pl.BlockSpec((1, ATTN_HEADS, 1, hd, tb), lambda i: (i // nblk, 0, i % nblk, 0, 0))
    return pl.pallas_call(
        _qk_norm_kernel,
        grid=(t // tb,),
        in_specs=[pl.BlockSpec((tb, d), lambda i: (i, 0)),
                  pl.BlockSpec((tb, d), lambda i: (i, 1)),
                  pl.BlockSpec((tb, d), lambda i: (i, 2)),
                  pl.BlockSpec((1, LANES), lambda i: (0, 0)),
                  pl.BlockSpec((1, LANES), lambda i: (0, 0))],
        out_specs=[t_spec, pl.BlockSpec((tb, d), lambda i: (i, 0)), t_spec],
        out_shape=[t_shape, jax.ShapeDtypeStruct((t, d), BF16), t_shape],
        compiler_params=_params("parallel"),
        name="qk_norm",
    )(qkv, qkv, qkv, qg, kg)


def _attn_kernel(lam_ref, q_ref, k_ref, v_ref, bias_ref, sg_ref, o_ref,
                 acc0, acc1, m0, l0, m1, l1, *, tb, out_scale):
    qi = pl.program_id(2)
    dim = lax.broadcasted_iota(jnp.int32, (2 * ATTN_HEAD_DIM, 1), 0)
    q = q_ref[0, 0, 0]
    zero = jnp.zeros_like(q)
    qs = (jnp.where(dim < ATTN_HEAD_DIM, q, zero), jnp.where(dim >= ATTN_HEAD_DIM, q, zero))
    state = ((acc0, m0, l0), (acc1, m1, l1))
    for acc, m, l in state:
        acc[...] = jnp.zeros_like(acc)
        m[...] = jnp.full_like(m, MASK_NEG)
        l[...] = jnp.zeros_like(l)

    def body(ki, carry):
        k = k_ref[0, pl.ds(pl.multiple_of(ki * tb, tb), tb), :]
        v = v_ref[0, 0, ki]
        bias = bias_ref[0, jnp.minimum(qi - ki, 2)]
        s = [jnp.dot(k, qm, preferred_element_type=F32) + bias for qm in qs]
        m_old = [m[...] for _, m, _ in state]
        m_new = [jnp.maximum(mo, jnp.max(si, axis=0, keepdims=True)) for mo, si in zip(m_old, s)]
        alpha = [jnp.exp(mo - mn) for mo, mn in zip(m_old, m_new)]
        p = [jnp.exp(si - mn) for si, mn in zip(s, m_new)]
        pv = [jnp.dot(v, pi.astype(BF16), preferred_element_type=F32) for pi in p]
        for i, (acc, m, l) in enumerate(state):
            l[...] = alpha[i] * l[...] + jnp.sum(p[i], axis=0, keepdims=True)
            acc[...] = alpha[i] * acc[...] + pv[i]
            m[...] = m_new[i]
        return carry

    lax.fori_loop(0, qi + 1, body, 0)
    lam = lam_ref[0]
    o = acc0[...] / l0[...] - lam * (acc1[...] / l1[...])
    ms = jnp.mean(o * o, axis=0, keepdims=True)
    o = o * lax.rsqrt(ms + EPS) * (sg_ref[...] * out_scale)
    o_ref[0] = jnp.transpose(o).astype(o_ref.dtype)


def _bucket_thresholds(max_dist):
    n = np.arange(max_dist, dtype=np.int32)
    max_exact = REL_BUCKETS // 2
    nf = np.maximum(n, 1).astype(np.float32)
    large = max_exact + (np.log(nf / np.float32(max_exact))
                         / np.float32(math.log(REL_MAX_DIST / max_exact))
                         * np.float32(REL_BUCKETS - max_exact)).astype(np.int32)
    bucket = np.where(n < max_exact, n, np.minimum(large, REL_BUCKETS - 1))
    assert bucket[-1] == REL_BUCKETS - 1 and np.all(np.diff(bucket) >= 0)
    return tuple(int(np.argmax(bucket >= k)) for k in range(1, REL_BUCKETS))


def _bias_tables_kernel(rb_ref, o_ref, *, tb, thresholds):
    h = pl.program_id(0)
    key = lax.broadcasted_iota(jnp.int32, (tb, tb), 0)
    query = lax.broadcasted_iota(jnp.int32, (tb, tb), 1)
    for j in range(2):
        dist = query - key + j * tb
        val = jnp.full((tb, tb), rb_ref[0, h], F32)
        for k, thr in enumerate(thresholds, start=1):
            val = jnp.where(dist >= thr, rb_ref[k, h], val)
        o_ref[0, j] = jnp.where(dist >= 0, val, MASK_NEG)
    o_ref[0, 2] = jnp.full((tb, tb), rb_ref[REL_BUCKETS - 1, h], F32)


def _bias_tables(rel_bias, tb):
    thresholds = _bucket_thresholds(2 * tb)
    assert thresholds[-1] <= tb + 1
    return pl.pallas_call(
        functools.partial(_bias_tables_kernel, tb=tb, thresholds=thresholds),
        grid=(ATTN_HEADS,),
        in_specs=[pl.BlockSpec(memory_space=pltpu.SMEM)],
        out_specs=pl.BlockSpec((1, 3, tb, tb), lambda h: (h, 0, 0, 0)),
        out_shape=jax.ShapeDtypeStruct((ATTN_HEADS, 3, tb, tb), F32),
        compiler_params=_params("parallel"),
        name="rel_bias_tables",
    )(rel_bias.astype(F32))


def diff_attention_core(q_t, k, v_t, bias, lam, sub_gain, lam_init):
    b, s, d = k.shape
    tb = bias.shape[-1]
    nblk = s // tb
    hd = 2 * ATTN_HEAD_DIM
    return pl.pallas_call(
        functools.partial(_attn_kernel, tb=tb, out_scale=1.0 - lam_init),
        grid=(b, ATTN_HEADS, nblk),
        in_specs=[pl.BlockSpec(memory_space=pltpu.SMEM),
                  pl.BlockSpec((1, 1, 1, hd, tb), lambda bi, h, qi: (bi, h, qi, 0, 0)),
                  pl.BlockSpec((1, s, hd), lambda bi, h, qi: (bi, 0, h)),
                  pl.BlockSpec((1, 1, nblk, hd, tb), lambda bi, h, qi: (bi, h, 0, 0, 0)),
                  pl.BlockSpec((1, 3, tb, tb), lambda bi, h, qi: (h, 0, 0, 0)),
                  pl.BlockSpec((hd, 1), lambda bi, h, qi: (0, 0))],
        out_specs=pl.BlockSpec((1, tb, hd), lambda bi, h, qi: (bi, qi, h)),
        out_shape=jax.ShapeDtypeStruct((b, s, d), BF16),
        scratch_shapes=[pltpu.VMEM((hd, tb), F32), pltpu.VMEM((hd, tb), F32),
                        pltpu.VMEM((1, tb), F32), pltpu.VMEM((1, tb), F32),
                        pltpu.VMEM((1, tb), F32), pltpu.VMEM((1, tb), F32)],
        compiler_params=_params("parallel", "parallel", "arbitrary"),
        name="diff_attention",
    )(lam.reshape(1).astype(F32), q_t, k, v_t, bias, sub_gain.astype(F32).reshape(hd, 1))


def _conv_gate_kernel(b_ref, c_ref, x_ref, w_ref, o_ref):
    z = c_ref[0] * x_ref[0]
    row = lax.broadcasted_iota(jnp.int32, z.shape, 0)
    z1 = jnp.where(row >= 1, pltpu.roll(z, 1, 0), 0.0)
    z2 = jnp.where(row >= 2, pltpu.roll(z, 2, 0), 0.0)
    zc = w_ref[0:1, :] * z2 + w_ref[1:2, :] * z1 + w_ref[2:3, :] * z
    o_ref[0] = (b_ref[0] * zc).astype(o_ref.dtype)


def conv_gate(bcx, conv_w):
    b, s, d3 = bcx.shape
    d = d3 // 3
    nd = d // LANES
    return pl.pallas_call(
        _conv_gate_kernel,
        grid=(b, nd),
        in_specs=[pl.BlockSpec((1, s, LANES), lambda bi, j: (bi, 0, j)),
                  pl.BlockSpec((1, s, LANES), lambda bi, j: (bi, 0, nd + j)),
                  pl.BlockSpec((1, s, LANES), lambda bi, j: (bi, 0, 2 * nd + j)),
                  pl.BlockSpec((3, LANES), lambda bi, j: (0, j))],
        out_specs=pl.BlockSpec((1, s, LANES), lambda bi, j: (bi, 0, j)),
        out_shape=jax.ShapeDtypeStruct((b, s, d), BF16),
        compiler_params=_params("parallel", "parallel"),
        name="conv_gate",
    )(bcx, bcx, bcx, conv_w.astype(F32))


def _sgu_gate_kernel(z_ref, vg_ref, ws_ref, bs_ref, o_ref):
    d = o_ref.shape[1]
    v = z_ref[:, d:]
    ms = jnp.mean(v * v, axis=-1, keepdims=True)
    vn = (v * lax.rsqrt(ms + EPS) * vg_ref[...]).astype(BF16)
    row = lax.broadcasted_iota(jnp.int32, (SGU_CHUNK, SGU_CHUNK), 0)
    col = lax.broadcasted_iota(jnp.int32, (SGU_CHUNK, SGU_CHUNK), 1)
    for g in range(SGU_GROUPS):
        sl = slice(g * LANES, (g + 1) * LANES)
        w = jnp.where(row >= col, ws_ref[g], 0.0).astype(BF16)
        sv = jnp.dot(w, vn[:, sl], preferred_element_type=F32) + bs_ref[:, g:g + 1]
        o_ref[:, sl] = (z_ref[:, sl] * sv).astype(o_ref.dtype)


def sgu_gate(z, v_gain, w_s, b_s):
    t, d2 = z.shape
    d = d2 // 2
    return pl.pallas_call(
        _sgu_gate_kernel,
        grid=(t // SGU_CHUNK,),
        in_specs=[pl.BlockSpec((SGU_CHUNK, d2), lambda i: (i, 0)),
                  pl.BlockSpec((1, d), lambda i: (0, 0)),
                  pl.BlockSpec((SGU_GROUPS, SGU_CHUNK, SGU_CHUNK), lambda i: (0, 0, 0)),
                  pl.BlockSpec((SGU_CHUNK, SGU_GROUPS), lambda i: (0, 0))],
        out_specs=pl.BlockSpec((SGU_CHUNK, d), lambda i: (i, 0)),
        out_shape=jax.ShapeDtypeStruct((t, d), BF16),
        compiler_params=_params("parallel"),
        name="sgu_gate",
    )(z, v_gain.astype(F32).reshape(1, d), w_s.astype(F32), jnp.transpose(b_s.astype(F32)))


def _norm_transpose_kernel(x_ref, g_ref, o_ref):
    x = x_ref[...]
    ms = jnp.mean(x * x, axis=-1, keepdims=True)
    o_ref[...] = _pack_bf16_pairs(jnp.transpose(x * lax.rsqrt(ms + EPS) * g_ref[...]))


def norm_transpose(x, g, *, tm=512):
    m, k = x.shape
    tm = min(tm, m)
    return pl.pallas_call(
        _norm_transpose_kernel,
        grid=(m // tm,),
        in_specs=[pl.BlockSpec((tm, k), lambda i: (i, 0)),
                  pl.BlockSpec((1, k), lambda i: (0, 0))],
        out_specs=pl.BlockSpec((k // 2, tm), lambda i: (0, i)),
        out_shape=jax.ShapeDtypeStruct((k // 2, m), jnp.uint32),
        compiler_params=_params("parallel"),
        name="norm_transpose",
    )(x, g.astype(F32).reshape(1, k))


_A_LAST_K1 = tuple(PEER_TOPK // (k2 + 1) - 1 for k2 in range(8))


def _top16(scores, t_refs):
    scores = list(scores)
    ranks = [jnp.full(s.shape, RANK_NONE, F32) for s in scores]
    for kk in range(PEER_TOPK):
        for c, t_ref in enumerate(t_refs):
            mx = jnp.max(scores[c], axis=0, keepdims=True)
            hit = scores[c] == mx
            ranks[c] = jnp.where(hit, float(kk), ranks[c])
            scores[c] = jnp.where(hit, KNOCKED, scores[c])
            t_ref[kk:kk + 1, :] = mx
    return ranks


def _pack_bf16_pairs(x):
    return pltpu.bitcast(x.astype(BF16), jnp.uint32)


def _candidates(t1, t2, row8):
    cand = [jnp.where(row8 <= _A_LAST_K1[k2], t1[0:8] + t2[k2:k2 + 1], KNOCKED) for k2 in range(8)]
    cand.append(t1[8:16] + t2[0:1])
    cand.append(t1[0:1] + t2[8:16])
    return cand


def _sixteenth_largest(cands):
    work = [list(c) for c in cands]
    thr = [None] * len(cands)
    for _ in range(PEER_TOPK):
        for i, w in enumerate(work):
            thr[i] = jnp.max(functools.reduce(jnp.maximum, w), axis=0, keepdims=True)
            work[i] = [jnp.where(x == thr[i], KNOCKED, x) for x in w]
    return thr


def _peer_select_kernel(q_ref, keys_ref, r2_ref, n1_ref, a_ref, b_ref, *t_refs):
    row8 = lax.broadcasted_iota(jnp.int32, (8, LANES), 0)

    def head_pair(hp, carry):
        heads = tuple(8 * hp + i for i in range(8))
        scores, ranks, tops = [], [], []
        for h in heads:
            base = pl.multiple_of(h * (2 * PEER_HALF), 2 * PEER_HALF)
            q1 = q_ref[pl.ds(base, PEER_HALF), :].astype(BF16)
            q2 = q_ref[pl.ds(base + PEER_HALF, PEER_HALF), :].astype(BF16)
            scores.append((jnp.dot(keys_ref[h, 0], q1, preferred_element_type=F32),
                           jnp.dot(keys_ref[h, 1], q2, preferred_element_type=F32)))
        for i in range(8):
            ranks.append(_top16(scores[i], t_refs[2 * i:2 * i + 2]))
            tops.append((t_refs[2 * i][...], t_refs[2 * i + 1][...]))
        cands = [_candidates(t1, t2, row8) for t1, t2 in tops]
        thrs = _sixteenth_largest(cands)

        for h, (s1, s2), (rank1, rank2), (t1, t2), cand, thr in zip(heads, scores, ranks, tops,
                                                                       cands, thrs):
            top = t1[0:1] + t2[0:1]
            sel = [c >= thr for c in cand]
            ex = [jnp.where(sl, jnp.exp(c - top), 0.0) for sl, c in zip(sel, cand)]
            z = jnp.sum(functools.reduce(jnp.add, ex), axis=0, keepdims=True)
            cnt = [jnp.where(sl, 1.0, 0.0) for sl in sel]
            n_lo = functools.reduce(jnp.add, cnt[:8])
            n_lo = n_lo + jnp.where(row8 == 0, jnp.sum(cnt[9], axis=0, keepdims=True), 0.0)
            n_hi = cnt[8]

            n1 = jnp.zeros(rank1.shape, F32)
            for kk in range(PEER_TOPK):
                src = n_lo if kk < 8 else n_hi
                n1 = jnp.where(rank1 == float(kk), src[kk % 8:kk % 8 + 1], n1)

            r2_ref[h] = _pack_bf16_pairs(rank2)
            n1_ref[h] = n1
            a_ref[h] = jnp.exp(s1 - t1[0:1])
            b_ref[h] = _pack_bf16_pairs(jnp.exp(s2 - t2[0:1]) / z)
        return carry

    lax.fori_loop(0, PEER_HEADS // 8, head_pair, 0)


def peer_select(q_t, keys):
    hq, t = q_t.shape
    shape = (PEER_HEADS, PEER_NKEYS, t)
    packed = (PEER_HEADS, PEER_NKEYS // 2, t)
    ospec = pl.BlockSpec((PEER_HEADS, PEER_NKEYS, LANES), lambda i: (0, 0, i))
    pspec = pl.BlockSpec((PEER_HEADS, PEER_NKEYS // 2, LANES), lambda i: (0, 0, i))
    return pl.pallas_call(
        _peer_select_kernel,
        grid=(t // LANES,),
        in_specs=[pl.BlockSpec((hq, LANES), lambda i: (0, i)),
                  pl.BlockSpec((PEER_HEADS, 2, PEER_NKEYS, PEER_HALF), lambda i: (0, 0, 0, 0))],
        out_specs=[pspec, ospec, ospec, pspec],
        out_shape=[jax.ShapeDtypeStruct(packed, jnp.uint32), jax.ShapeDtypeStruct(shape, F32),
                   jax.ShapeDtypeStruct(shape, F32), jax.ShapeDtypeStruct(packed, jnp.uint32)],
        scratch_shapes=[pltpu.VMEM((PEER_TOPK, LANES), F32)] * 16,
        compiler_params=_params("parallel"),
        name="peer_select",
    )(q_t, keys)


def _gated_rows(at_ref, y_ref, r2_ref, b_ref, n1_ref, a_ref, r0):
    pieces = PEER_NKEYS // BF16_ROWS
    zero = jnp.zeros((BF16_ROWS, LANES), BF16)
    rows = tuple(range(r0, r0 + MXU_ROWS // PEER_NKEYS))

    def bcast_row(ref, h, r, ls):
        return jnp.broadcast_to(ref[h, r:r + 1, ls], (BF16_ROWS, LANES)).astype(BF16)

    for lt in range(at_ref.shape[1] // LANES):
        ls = slice(lt * LANES, (lt + 1) * LANES)
        w = [[zero] * pieces for _ in rows]
        for h in range(PEER_HEADS):
            n1 = [bcast_row(n1_ref, h, r, ls) for r in rows]
            a1 = [bcast_row(a_ref, h, r, ls) for r in rows]
            for p in range(pieces):
                ks = slice(p * BF16_ROWS // 2, (p + 1) * BF16_ROWS // 2)
                r2 = pltpu.bitcast(r2_ref[h, ks, ls], BF16)
                bb = pltpu.bitcast(b_ref[h, ks, ls], BF16)
                for i in range(len(rows)):
                    w[i][p] = w[i][p] + jnp.where(r2 < n1[i], bb, zero) * a1[i]
        for i, r in enumerate(rows):
            for p in range(pieces):
                r_lo = r * PEER_NKEYS + p * BF16_ROWS
                y = _gelu(at_ref[r_lo:r_lo + BF16_ROWS, ls].astype(BF16)) * w[i][p]
                y_ref[r_lo // 2:(r_lo + BF16_ROWS) // 2, ls] = pltpu.bitcast(y, jnp.uint32)


def _peer_dense_kernel(x_ref, u_ref, vt_ref, r2_ref, b_ref, n1_ref, a_ref, res_ref, o_ref,
                       acc_ref, at_ref, y_ref):
    j = pl.program_id(1)
    x = pltpu.bitcast(x_ref[...], BF16)
    n_pieces = at_ref.shape[0] // MXU_ROWS
    for i in range(n_pieces):
        piece = slice(i * MXU_ROWS, (i + 1) * MXU_ROWS)
        words = slice(i * MXU_ROWS // 2, (i + 1) * MXU_ROWS // 2)
        u = pltpu.bitcast(u_ref[words, :], BF16)
        at_ref[piece, :] = jnp.dot(u, x, preferred_element_type=F32)
    acc = jnp.where(j == 0, 0.0, acc_ref[...])
    for i in range(n_pieces):
        piece = slice(i * MXU_ROWS, (i + 1) * MXU_ROWS)
        words = slice(i * MXU_ROWS // 2, (i + 1) * MXU_ROWS // 2)
        _gated_rows(at_ref, y_ref, r2_ref, b_ref, n1_ref, a_ref, i * (MXU_ROWS // PEER_NKEYS))
        vt = pltpu.bitcast(vt_ref[:, piece], BF16)
        y = pltpu.bitcast(y_ref[words, :], BF16)
        acc = acc + jnp.dot(vt, y, preferred_element_type=F32)
    acc_ref[...] = acc

    @pl.when(j == pl.num_programs(1) - 1)
    def _():
        o_ref[...] = res_ref[...] + jnp.transpose(acc_ref[...])


def peer_dense(xn_t, u, v_t, r2, b, n1, a, res, *, tt=512, ec=2048):
    d, t = res.shape[1], res.shape[0]
    e = v_t.shape[1]
    tt = min(tt, t)
    rows = ec // PEER_NKEYS
    assert t % tt == 0 and e % ec == 0 and ec % MXU_ROWS == 0
    sel_full = pl.BlockSpec((PEER_HEADS, PEER_NKEYS // 2, tt), lambda i, j: (0, 0, i))
    sel_rows = pl.BlockSpec((PEER_HEADS, rows, tt), lambda i, j: (0, j, i))
    return pl.pallas_call(
        _peer_dense_kernel,
        grid=(t // tt, e // ec),
        in_specs=[pl.BlockSpec((d // 2, tt), lambda i, j: (0, i)),
                  pl.BlockSpec((ec // 2, d), lambda i, j: (j, 0)),
                  pl.BlockSpec((d // 2, ec), lambda i, j: (0, j)),
                  sel_full, sel_full, sel_rows, sel_rows,
                  pl.BlockSpec((tt, d), lambda i, j: (i, 0))],
        out_specs=pl.BlockSpec((tt, d), lambda i, j: (i, 0)),
        out_shape=jax.ShapeDtypeStruct((t, d), F32),
        scratch_shapes=[pltpu.VMEM((d, tt), F32), pltpu.VMEM((ec, tt), F32),
                        pltpu.VMEM((ec // 2, tt), jnp.uint32)],
        compiler_params=_params("parallel", "arbitrary"),
        name="peer_dense",
    )(xn_t, u, v_t, r2, b, n1, a, res)


def peer_ffn(h2, g, w_q, keys, u, v, layer):
    xn_t = norm_transpose(h2, g)
    q_t = matmul(jnp.transpose(w_q).astype(BF16), xn_t)
    r2, n1, a, b = peer_select(q_t, keys.astype(BF16))
    return peer_dense(xn_t, cast_pack(u, layer), cast_pack(v, layer, transpose=True),
                      r2, b, n1, a, h2)


def _lambda_init(layer_idx):
    return 0.8 - 0.6 * math.exp(-0.3 * layer_idx)


def kernel(x, rel_bias, norm_mix, norm_ffn, attn_w_qkv, attn_q_gain, attn_k_gain, attn_lam_q1,
           attn_lam_k1, attn_lam_q2, attn_lam_k2, attn_sub_gain, attn_w_o, conv_w_in, conv_w,
           conv_w_out, sgu_w_in, sgu_v_gain, sgu_w_s, sgu_b_s, sgu_w_out, peer_w_q, peer_keys,
           peer_u, peer_v):
    bsz, seq, d = x.shape
    t = bsz * seq
    h = x.reshape(t, d).astype(F32)
    depth = norm_mix.shape[0]
    bias = _bias_tables(rel_bias, min(ATTN_BLOCK, seq))
    for i in range(depth):
        m, j = i % N_MIXERS, i // N_MIXERS
        g = norm_mix[i].astype(F32)
        if m == 0:
            qkv = norm_matmul(h, g, attn_w_qkv[j].astype(BF16))
            q_t, k, v_t = qk_norm(qkv, attn_q_gain[j], attn_k_gain[j], bsz, bias.shape[-1])
            lam_init = _lambda_init(i)
            lam = (jnp.exp(jnp.sum(attn_lam_q1[j].astype(F32) * attn_lam_k1[j].astype(F32)))
                   - jnp.exp(jnp.sum(attn_lam_q2[j].astype(F32) * attn_lam_k2[j].astype(F32)))
                   + lam_init)
            o = diff_attention_core(q_t, k.reshape(bsz, seq, d), v_t, bias, lam,
                                    attn_sub_gain[j], lam_init)
            h = matmul_residual(o.reshape(t, d), attn_w_o[j].astype(BF16), h)
        elif m == 1:
            bcx = norm_matmul(h, g, conv_w_in[j].astype(BF16))
            gated = conv_gate(bcx.reshape(bsz, seq, 3 * d), conv_w[j])
            h = matmul_residual(gated.reshape(t, d), conv_w_out[j].astype(BF16), h)
        else:
            z = norm_matmul(h, g, sgu_w_in[j].astype(BF16), act="gelu")
            gated = sgu_gate(z, sgu_v_gain[j], sgu_w_s[j], sgu_b_s[j])
            h = matmul_residual(gated, sgu_w_out[j].astype(BF16), h)
        h = peer_ffn(h, norm_ffn[i], peer_w_q[i], peer_keys[i], peer_u, peer_v, i)
    return h.reshape(bsz, seq, d).astype(x.dtype)
```
